```python
import math
import jax, jax.numpy as jnp
from jax import lax
import numpy as np

D_MODEL = 1024
BATCH = 4
SEQ = 8192
DEPTH = 2

CHUNK = 64
N_MIXERS = 2
N_POOL_LAYERS = (DEPTH + 1) // 2
N_ATTN_LAYERS = DEPTH // 2
RMS_EPS = 1e-6
POOL_WINDOWS = (2, 4, 8, 16)
POOL_GROUPS = len(POOL_WINDOWS)
POOL_GW = D_MODEL // POOL_GROUPS
N_HEADS = 8
HEAD_DIM = 64
D_ATTN = N_HEADS * 2 * HEAD_DIM
Q_BLOCK = 128
ATTN_SCALE = HEAD_DIM ** -0.5
SUBLN_EPS = 1e-5
NEG_INF = -1e30
N_GROUPS = 4
EXPERTS_PER_GROUP = 8
N_EXPERTS = N_GROUPS * EXPERTS_PER_GROUP
TOP_K = 2
D_EXPERT = 512
DISPATCH_BLOCK = 256

kernel_name = "hybrid_pool_diffattn_hiermoe"


def rms_norm(x, g, eps=RMS_EPS):
    xf = x.astype(jnp.float32)
    y = xf * lax.rsqrt(jnp.mean(xf * xf, axis=-1, keepdims=True) + eps)
    return (y * g.astype(jnp.float32)).astype(x.dtype)


def alibi_slopes(n_heads):
    return jnp.array([2.0 ** (-8.0 * (h + 1) / n_heads) for h in range(n_heads)], dtype=jnp.float32)


def pool_mixer(h, w, scale):
    B, S, D = h.shape
    hf = h.astype(jnp.float32)
    outs = []
    for gi, win in enumerate(POOL_WINDOWS):
        hg = hf[..., gi * POOL_GW:(gi + 1) * POOL_GW]
        cs = jnp.cumsum(hg, axis=1)
        lagged = jnp.pad(cs[:, :-win], ((0, 0), (win, 0), (0, 0)))
        cnt = jnp.minimum(jnp.arange(S) + 1, win).astype(jnp.float32)
        mix = (cs - lagged) / cnt[None, :, None] - hg
        outs.append(jnp.einsum('bsc,cd->bsd', mix.astype(h.dtype), w[gi]))
    return jnp.concatenate(outs, axis=-1) * scale


def diff_attention(h, wq, wk, wv, wo, lq1, lk1, lq2, lk2, subln_g, layer_idx):
    B, S, _ = h.shape
    nb = S // Q_BLOCK
    q = (h @ wq).reshape(B, S, N_HEADS, 2, HEAD_DIM)
    k = (h @ wk).reshape(B, S, N_HEADS, 2, HEAD_DIM)
    v = (h @ wv).reshape(B, S, N_HEADS, 2 * HEAD_DIM)
    lam_init = 0.8 - 0.6 * math.exp(-0.3 * layer_idx)
    f32 = jnp.float32
    lam = (jnp.exp(jnp.sum(lq1.astype(f32) * lk1.astype(f32)))
           - jnp.exp(jnp.sum(lq2.astype(f32) * lk2.astype(f32))) + lam_init)
    slopes = alibi_slopes(N_HEADS)
    sk = jnp.arange(S)
    k_chunk = sk // CHUNK
    q_blocks = q.reshape(B, nb, Q_BLOCK, N_HEADS, 2, HEAD_DIM).transpose(1, 0, 2, 3, 4, 5)

    def one_block(args):
        qb, bi = args
        tq = bi * Q_BLOCK + jnp.arange(Q_BLOCK)
        s = jnp.einsum('bqhcd,bkhcd->bhcqk', qb, k).astype(f32) * ATTN_SCALE
        dist = jnp.abs(tq[:, None] - sk[None, :]).astype(f32)
        allowed = k_chunk[None, :] <= (tq // CHUNK)[:, None]
        bias = jnp.where(allowed[None], -slopes[:, None, None] * dist[None], NEG_INF)
        p = jax.nn.softmax(s + bias[None, :, None], axis=-1)
        a = (p[:, :, 0] - lam * p[:, :, 1]).astype(v.dtype)
        return jnp.einsum('bhqk,bkhe->bqhe', a, v)

    o = lax.map(one_block, (q_blocks, jnp.arange(nb)))
    o = o.transpose(1, 0, 2, 3, 4).reshape(B, S, N_HEADS, 2 * HEAD_DIM)
    o = rms_norm(o, subln_g, SUBLN_EPS) * (1.0 - lam_init)
    return o.reshape(B, S, D_ATTN) @ wo


def hier_moe(h, wg, bg, we, be, w1, w3, w2):
    B, S, D = h.shape
    N = B * S
    xf = h.reshape(N, D)
    f32 = jnp.float32
    g_prob = jax.nn.softmax((xf @ wg).astype(f32) + bg.astype(f32), axis=-1)
    g_p, g_idx = lax.top_k(g_prob, 1)
    e_logits = ((xf @ we).astype(f32) + be.astype(f32)).reshape(N, N_GROUPS, EXPERTS_PER_GROUP)
    e_in = jnp.take_along_axis(e_logits, g_idx[:, :, None], axis=1)[:, 0]
    top_v, top_i = lax.top_k(e_in, TOP_K)
    gate = g_p * jax.nn.softmax(top_v, axis=-1)
    e_flat = (g_idx * EXPERTS_PER_GROUP + top_i).reshape(-1)
    tok_flat = jnp.repeat(jnp.arange(N), TOP_K)
    w_flat = gate.reshape(-1)
    NK = N * TOP_K
    order = jnp.argsort(e_flat)
    e_sorted = e_flat[order]
    tok_sorted = tok_flat[order]
    w_sorted = w_flat[order]
    counts = jnp.bincount(e_flat, length=N_EXPERTS)
    padded = ((counts + DISPATCH_BLOCK - 1) // DISPATCH_BLOCK) * DISPATCH_BLOCK
    pad_end = jnp.cumsum(padded)
    pad_start = pad_end - padded
    off_start = jnp.cumsum(counts) - counts
    dest = pad_start[e_sorted] + jnp.arange(NK) - off_start[e_sorted]
    n_slots = NK + N_EXPERTS * DISPATCH_BLOCK
    n_blocks = n_slots // DISPATCH_BLOCK
    slot_tok = jnp.zeros((n_slots,), jnp.int32).at[dest].set(tok_sorted)
    slot_w = jnp.zeros((n_slots,), h.dtype).at[dest].set(w_sorted.astype(h.dtype))
    block_expert = jnp.minimum(
        jnp.searchsorted(pad_end, jnp.arange(n_blocks) * DISPATCH_BLOCK, side='right'),
        N_EXPERTS - 1)
    xs = xf[slot_tok].reshape(n_blocks, DISPATCH_BLOCK, D)

    def expert_block(args):
        xb, e = args
        a = jax.nn.silu(xb @ w1[e]) * (xb @ w3[e])
        return a @ w2[e]

    ys = lax.map(expert_block, (xs, block_expert)).reshape(n_slots, D)
    y = jnp.zeros((N, D), h.dtype).at[slot_tok].add(ys * slot_w[:, None])
    return y.reshape(B, S, D)


def setup_inputs(seed: int = 0) -> dict:
    key = jax.random.key(seed)
    ks = jax.random.split(key, 24)
    f32 = jnp.float32
    D = D_MODEL

    def nrm(k, shape, scale):
        return jax.random.normal(k, shape, f32) * scale

    return {
        "x": nrm(ks[0], (BATCH, SEQ, D), 1.0),
        "norm_mix_g": 1.0 + nrm(ks[1], (DEPTH, D), 0.02),
        "norm_ffn_g": 1.0 + nrm(ks[2], (DEPTH, D), 0.02),
        "pool_w": nrm(ks[3], (N_POOL_LAYERS, POOL_GROUPS, POOL_GW, POOL_GW), POOL_GW ** -0.5),
        "pool_scale": 1.0 + nrm(ks[4], (N_POOL_LAYERS, D), 0.02),
        "attn_wq": nrm(ks[5], (N_ATTN_LAYERS, D, D_ATTN), D ** -0.5),
        "attn_wk": nrm(ks[6], (N_ATTN_LAYERS, D, D_ATTN), D ** -0.5),
        "attn_wv": nrm(ks[7], (N_ATTN_LAYERS, D, D_ATTN), D ** -0.5),
        "attn_wo": nrm(ks[8], (N_ATTN_LAYERS, D_ATTN, D), D_ATTN ** -0.5),
        "attn_lq1": nrm(ks[9], (N_ATTN_LAYERS, HEAD_DIM), 0.1),
        "attn_lk1": nrm(ks[10], (N_ATTN_LAYERS, HEAD_DIM), 0.1),
        "attn_lq2": nrm(ks[11], (N_ATTN_LAYERS, HEAD_DIM), 0.1),
        "attn_lk2": nrm(ks[12], (N_ATTN_LAYERS, HEAD_DIM), 0.1),
        "attn_subln_g": 1.0 + nrm(ks[13], (N_ATTN_LAYERS, 2 * HEAD_DIM), 0.02),
        "moe_wg": nrm(ks[14], (DEPTH, D, N_GROUPS), D ** -0.5),
        "moe_bg": nrm(ks[15], (DEPTH, N_GROUPS), 0.01),
        "moe_we": nrm(ks[16], (DEPTH, D, N_EXPERTS), D ** -0.5),
        "moe_be": nrm(ks[17], (DEPTH, N_EXPERTS), 0.01),
        "moe_w1": nrm(ks[18], (DEPTH, N_EXPERTS, D, D_EXPERT), D ** -0.5),
        "moe_w3": nrm(ks[19], (DEPTH, N_EXPERTS, D, D_EXPERT), D ** -0.5),
        "moe_w2": nrm(ks[20], (DEPTH, N_EXPERTS, D_EXPERT, D), D_EXPERT ** -0.5),
        "final_g": 1.0 + nrm(ks[21], (D,), 0.02),
    }


def reference(x, norm_mix_g, norm_ffn_g, pool_w, pool_scale, attn_wq, attn_wk, attn_wv, attn_wo,
              attn_lq1, attn_lk1, attn_lq2, attn_lk2, attn_subln_g, moe_wg, moe_bg, moe_we, moe_be,
              moe_w1, moe_w3, moe_w2, final_g):
    for i in range(DEPTH):
        h = rms_norm(x, norm_mix_g[i])
        j = i // N_MIXERS
        if i % N_MIXERS == 0:
            x = x + pool_mixer(h, pool_w[j], pool_scale[j])
        else:
            x = x + diff_attention(h, attn_wq[j], attn_wk[j], attn_wv[j], attn_wo[j],
                                   attn_lq1[j], attn_lk1[j], attn_lq2[j], attn_lk2[j],
                                   attn_subln_g[j], i)
        h = rms_norm(x, norm_ffn_g[i])
        x = x + hier_moe(h, moe_wg[i], moe_bg[i], moe_we[i], moe_be[i],
                         moe_w1[i], moe_w3[i], moe_w2[i])
    return rms_norm(x, final_g)
```

```python
import functools
import math

import jax
import jax.numpy as jnp
from jax import lax
from jax.experimental import pallas as pl
from jax.experimental.pallas import tpu as pltpu

F32 = jnp.float32
BF16 = jnp.bfloat16

RMS_EPS = 1e-6
SUBLN_EPS = 1e-5
POOL_WINDOWS = (2, 4, 8, 16)
CHUNK = 64
CHUNK_SHIFT = 6
N_HEADS = 8
HEAD_DIM = 64
N_GROUPS = 4
EXPERTS_PER_GROUP = 8
N_EXPERTS = N_GROUPS * EXPERTS_PER_GROUP
NEG_INF = -1e30

VMEM_LIMIT_BYTES = 48 * 1024 * 1024
POOL_TILE = 256
ROW_TILE = 512
EXPERT_BLOCK = 256
ATTN_TQ = 256
ATTN_TK = 256
LOGIT_ROWS = 40


def _cparams(sem):
    return pltpu.CompilerParams(dimension_semantics=sem, vmem_limit_bytes=VMEM_LIMIT_BYTES)


def _rms(x, g, eps):
    return x * lax.rsqrt(jnp.mean(x * x, axis=-1, keepdims=True) + eps) * g


def _pool_kernel(x_ref, g_ref, band_ref, icnt_ref, w_ref, sc_ref, o_ref, hh_ref):
    T = POOL_TILE
    si = pl.program_id(1)

    @pl.when(si == 0)
    def _():
        hh_ref[0:T, :] = jnp.zeros((T, hh_ref.shape[1]), BF16)

    @pl.when(si > 0)
    def _():
        hh_ref[0:T, :] = hh_ref[T:2 * T, :]

    x = x_ref[...]
    h = _rms(x, g_ref[...], RMS_EPS)
    hh_ref[T:2 * T, :] = h.astype(BF16)
    gw = w_ref.shape[1]
    outs = []
    for gi in range(len(POOL_WINDOWS)):
        cols = slice(gi * gw, (gi + 1) * gw)
        wsum = jnp.dot(band_ref[gi], hh_ref[:, cols], preferred_element_type=F32)
        mix = wsum * icnt_ref[gi] - h[:, cols]
        outs.append(jnp.dot(mix.astype(BF16), w_ref[gi], preferred_element_type=F32))
    o_ref[...] = x + jnp.concatenate(outs, axis=-1) * sc_ref[...]


def _pool_layer(x, g, w, scale):
    B, S, D = x.shape
    T = POOL_TILE
    G = len(POOL_WINDOWS)
    r = jnp.arange(T)[:, None]
    c = jnp.arange(2 * T)[None, :]
    band = jnp.stack([((c <= r + T) & (c > r + T - win)) for win in POOL_WINDOWS]).astype(BF16)
    pos = jnp.arange(S, dtype=jnp.int32)
    icnt = jnp.stack([1.0 / jnp.minimum(pos + 1, win).astype(F32) for win in POOL_WINDOWS])
    icnt = icnt.reshape(G, S, 1)
    return pl.pallas_call(
        _pool_kernel,
        out_shape=jax.ShapeDtypeStruct((B, S, D), F32),
        grid=(B, S // T),
        in_specs=[
            pl.BlockSpec((None, T, D), lambda b, s: (b, s, 0)),
            pl.BlockSpec((1, D), lambda b, s: (0, 0)),
            pl.BlockSpec((G, T, 2 * T), lambda b, s: (0, 0, 0)),
            pl.BlockSpec((G, T, 1), lambda b, s: (0, s, 0)),
            pl.BlockSpec((G, D // G, D // G), lambda b, s: (0, 0, 0)),
            pl.BlockSpec((1, D), lambda b, s: (0, 0)),
        ],
        out_specs=pl.BlockSpec((None, T, D), lambda b, s: (b, s, 0)),
        scratch_shapes=[pltpu.VMEM((2 * T, D), BF16)],
        compiler_params=_cparams(("arbitrary", "arbitrary")),
        name="pool_mixer",
    )(x, g.reshape(1, D), band, icnt, w.astype(BF16), scale.reshape(1, D))


def _router_kernel(x_ref, g_ref, wt_ref, b_ref, h_ref, e_ref, gate_ref, cnt_ref):
    i = pl.program_id(0)
    x = x_ref[...]
    h = _rms(x, g_ref[...], RMS_EPS)
    h_ref[...] = h
    lt = lax.dot_general(wt_ref[...], h, (((1,), (1,)), ((), ())),
                         precision=lax.Precision.HIGHEST, preferred_element_type=F32)
    lt = lt + b_ref[...]
    g0, g1, g2, g3 = lt[0:1], lt[1:2], lt[2:3], lt[3:4]
    gmax = jnp.maximum(jnp.maximum(g0, g1), jnp.maximum(g2, g3))
    gidx = jnp.where(g0 == gmax, 0, jnp.where(g1 == gmax, 1, jnp.where(g2 == gmax, 2, 3)))
    gsum = jnp.exp(g0 - gmax) + jnp.exp(g1 - gmax) + jnp.exp(g2 - gmax) + jnp.exp(g3 - gmax)
    g_p = 1.0 / gsum
    E = EXPERTS_PER_GROUP
    e_in = jnp.where(gidx == 0, lt[8:8 + E],
                     jnp.where(gidx == 1, lt[8 + E:8 + 2 * E],
                               jnp.where(gidx == 2, lt[8 + 2 * E:8 + 3 * E], lt[8 + 3 * E:8 + 4 * E])))
    io = lax.broadcasted_iota(jnp.int32, e_in.shape, 0)
    v0 = jnp.max(e_in, axis=0, keepdims=True)
    i0 = jnp.min(jnp.where(e_in == v0, io, E), axis=0, keepdims=True)
    rest = jnp.where(io == i0, -jnp.inf, e_in)
    v1 = jnp.max(rest, axis=0, keepdims=True)
    i1 = jnp.min(jnp.where(rest == v1, io, E), axis=0, keepdims=True)
    ex = jnp.exp(v1 - v0)
    p0 = 1.0 / (1.0 + ex)
    e0 = gidx * E + i0
    e1 = gidx * E + i1
    e_ref[0:1, :] = e0
    e_ref[1:2, :] = e1
    gate_ref[0:1, :] = g_p * p0
    gate_ref[1:2, :] = g_p * (ex * p0)
    io32 = lax.broadcasted_iota(jnp.int32, (N_EXPERTS, x.shape[0]), 0)
    oh = (io32 == e0).astype(F32) + (io32 == e1).astype(F32)
    tile_cnt = jnp.sum(oh, axis=1, keepdims=True)

    @pl.when(i == 0)
    def _():
        cnt_ref[...] = jnp.zeros(cnt_ref.shape, F32)

    cnt_ref[...] += jnp.broadcast_to(tile_cnt, cnt_ref.shape)


def _router(x2d, g, wg, bg, we, be):
    N, D = x2d.shape
    TM = ROW_TILE
    wt = jnp.zeros((LOGIT_ROWS, D), F32).at[0:N_GROUPS].set(wg.T).at[8:8 + N_EXPERTS].set(we.T)
    bt = jnp.zeros((LOGIT_ROWS, 1), F32).at[0:N_GROUPS, 0].set(bg).at[8:8 + N_EXPERTS, 0].set(be)
    return pl.pallas_call(
        _router_kernel,
        out_shape=(
            jax.ShapeDtypeStruct((N, D), F32),
            jax.ShapeDtypeStruct((2, N), jnp.int32),
            jax.ShapeDtypeStruct((2, N), F32),
            jax.ShapeDtypeStruct((N_EXPERTS, 128), F32),
        ),
        grid=(N // TM,),
        in_specs=[
            pl.BlockSpec((TM, D), lambda i: (i, 0)),
            pl.BlockSpec((1, D), lambda i: (0, 0)),
            pl.BlockSpec((LOGIT_ROWS, D), lambda i: (0, 0)),
            pl.BlockSpec((LOGIT_ROWS, 1), lambda i: (0, 0)),
        ],
        out_specs=(
            pl.BlockSpec((TM, D), lambda i: (i, 0)),
            pl.BlockSpec((2, TM), lambda i: (0, i)),
            pl.BlockSpec((2, TM), lambda i: (0, i)),
            pl.BlockSpec((N_EXPERTS, 128), lambda i: (0, 0)),
        ),
        compiler_params=_cparams(("arbitrary",)),
        name="moe_router",
    )(x2d, g.reshape(1, D), wt, bt)


def _rank_kernel(e_ref, start_ref, tri_ref, dest_ref, carry_ref):
    i = pl.program_id(0)

    @pl.when(i == 0)
    def _():
        carry_ref[...] = jnp.zeros(carry_ref.shape, F32)

    TM = e_ref.shape[1]
    io32 = lax.broadcasted_iota(jnp.int32, (N_EXPERTS, TM), 0)
    base = carry_ref[...] + start_ref[...]
    for k in range(2):
        oh = io32 == e_ref[k:k + 1, :]
        ohf = oh.astype(F32)
        before = jnp.dot(ohf.astype(BF16), tri_ref[...], preferred_element_type=F32)
        slot = jnp.sum(jnp.where(oh, before + base, 0.0), axis=0, keepdims=True)
        dest_ref[k:k + 1, :] = slot.astype(jnp.int32)
        base = base + jnp.sum(ohf, axis=1, keepdims=True)
    carry_ref[...] = base - start_ref[...]


def _rank(e_t, seg_start):
    _, N = e_t.shape
    TM = ROW_TILE
    tri = (jnp.arange(TM)[:, None] < jnp.arange(TM)[None, :]).astype(BF16)
    return pl.pallas_call(
        _rank_kernel,
        out_shape=jax.ShapeDtypeStruct((2, N), jnp.int32),
        grid=(N // TM,),
        in_specs=[
            pl.BlockSpec((2, TM), lambda i: (0, i)),
            pl.BlockSpec((N_EXPERTS, 1), lambda i: (0, 0)),
            pl.BlockSpec((TM, TM), lambda i: (0, 0)),
        ],
        out_specs=pl.BlockSpec((2, TM), lambda i: (0, i)),
        scratch_shapes=[pltpu.VMEM((N_EXPERTS, 1), F32)],
        compiler_params=_cparams(("arbitrary",)),
        name="moe_rank",
    )(e_t, seg_start.astype(F32).reshape(N_EXPERTS, 1), tri)


def _row_copy(src_ref, src_row, dst_ref, dst_row, sem):
    return pltpu.make_async_copy(src_ref.at[pl.ds(src_row, 1)], dst_ref.at[pl.ds(dst_row, 1)], sem)


def _dispatch_kernel(dest_ref, h_ref, xs_in_ref, xs_ref, sem):
    del xs_in_ref
    TM = h_ref.shape[0]

    def issue(t, carry):
        _row_copy(h_ref, t, xs_ref, dest_ref[0, 0, t], sem).start()
        _row_copy(h_ref, t, xs_ref, dest_ref[0, 1, t], sem).start()
        return carry

    lax.fori_loop(0, TM, issue, 0, unroll=8)
    for _ in range(2):
        pltpu.make_async_copy(h_ref, xs_ref.at[pl.ds(0, TM)], sem).wait()


def _dispatch(h, dest_t, n_slots):
    N, D = h.shape
    TM = ROW_TILE
    dest3 = dest_t.reshape(2, N // TM, TM).transpose(1, 0, 2)
    xs0 = jnp.zeros((n_slots, D), h.dtype)
    return pl.pallas_call(
        _dispatch_kernel,
        out_shape=jax.ShapeDtypeStruct((n_slots, D), h.dtype),
        grid=(N // TM,),
        in_specs=[
            pl.BlockSpec((1, 2, TM), lambda i: (i, 0, 0), memory_space=pltpu.SMEM),
            pl.BlockSpec((TM, D), lambda i: (i, 0)),
            pl.BlockSpec(memory_space=pl.ANY),
        ],
        out_specs=pl.BlockSpec(memory_space=pl.ANY),
        scratch_shapes=[pltpu.SemaphoreType.DMA],
        input_output_aliases={2: 0},
        compiler_params=_cparams(("arbitrary",)),
        name="moe_dispatch",
    )(dest3, h, xs0)


def _expert_kernel(be_ref, nu_ref, xs_ref, w1_ref, w3_ref, w2_ref, ys_ref, w1b, w3b, w2b):
    i = pl.program_id(0)

    @pl.when(i < nu_ref[0])
    def _():
        prev = be_ref[jnp.maximum(i - 1, 0)]

        @pl.when((i == 0) | (be_ref[i] != prev))
        def _():
            w1b[...] = w1_ref[0].astype(BF16)
            w3b[...] = w3_ref[0].astype(BF16)
            w2b[...] = w2_ref[0].astype(BF16)

        xb = xs_ref[...].astype(BF16)
        a = jnp.dot(xb, w1b[...], preferred_element_type=F32)
        b = jnp.dot(xb, w3b[...], preferred_element_type=F32)
        act = a * (1.0 / (1.0 + jnp.exp(-a))) * b
        ys_ref[...] = jnp.dot(act.astype(BF16), w2b[...], preferred_element_type=F32)

    @pl.when(i >= nu_ref[0])
    def _():
        ys_ref[...] = jnp.zeros(ys_ref.shape, F32)


def _experts(xs, block_expert, n_used, w1, w3, w2):
    n_slots, D = xs.shape
    DE = w1.shape[2]
    BLK = EXPERT_BLOCK
    n_blocks = n_slots // BLK

    def xmap(i, be, nu):
        return (jnp.minimum(i, nu[0] - 1), 0)

    def wmap(i, be, nu):
        return (be[i], 0, 0)

    return pl.pallas_call(
        _expert_kernel,
        out_shape=jax.ShapeDtypeStruct((n_slots, D), F32),
        grid_spec=pltpu.PrefetchScalarGridSpec(
            num_scalar_prefetch=2,
            grid=(n_blocks,),
            in_specs=[
                pl.BlockSpec((BLK, D), xmap),
                pl.BlockSpec((1, D, DE), wmap),
                pl.BlockSpec((1, D, DE), wmap),
                pl.BlockSpec((1, DE, D), wmap),
            ],
            out_specs=pl.BlockSpec((BLK, D), lambda i, be, nu: (i, 0)),
            scratch_shapes=[
                pltpu.VMEM((D, DE), BF16),
                pltpu.VMEM((D, DE), BF16),
                pltpu.VMEM((DE, D), BF16),
            ],
        ),
        compiler_params=_cparams(("arbitrary",)),
        name="moe_experts",
    )(block_expert, n_used, xs, w1, w3, w2)


def _combine_kernel(dest_ref, x_ref, gate_ref, ys_ref, g_ref, o_ref, buf0, buf1, sem, *, final_norm):
    TM = x_ref.shape[0]

    def issue(t, carry):
        _row_copy(ys_ref, dest_ref[0, 0, t], buf0, t, sem).start()
        _row_copy(ys_ref, dest_ref[0, 1, t], buf1, t, sem).start()
        return carry

    lax.fori_loop(0, TM, issue, 0, unroll=8)
    pltpu.make_async_copy(ys_ref.at[pl.ds(0, TM)], buf0, sem).wait()
    pltpu.make_async_copy(ys_ref.at[pl.ds(0, TM)], buf1, sem).wait()
    gate = gate_ref[...]
    y = x_ref[...] + gate[:, 0:1] * buf0[...] + gate[:, 1:2] * buf1[...]
    if final_norm:
        y = _rms(y, g_ref[...], RMS_EPS)
    o_ref[...] = y


def _combine(x2d, dest_t, gate_t, ys, g, final_norm):
    N, D = x2d.shape
    TM = ROW_TILE
    dest3 = dest_t.reshape(2, N // TM, TM).transpose(1, 0, 2)
    gate = gate_t.T
    return pl.pallas_call(
        functools.partial(_combine_kernel, final_norm=final_norm),
        out_shape=jax.ShapeDtypeStruct((N, D), F32),
        grid=(N // TM,),
        in_specs=[
            pl.BlockSpec((1, 2, TM), lambda i: (i, 0, 0), memory_space=pltpu.SMEM),
            pl.BlockSpec((TM, D), lambda i: (i, 0)),
            pl.BlockSpec((TM, 2), lambda i: (i, 0)),
            pl.BlockSpec(memory_space=pl.ANY),
            pl.BlockSpec((1, D), lambda i: (0, 0)),
        ],
        out_specs=pl.BlockSpec((TM, D), lambda i: (i, 0)),
        scratch_shapes=[pltpu.VMEM((TM, D), F32), pltpu.VMEM((TM, D), F32), pltpu.SemaphoreType.DMA],
        compiler_params=_cparams(("arbitrary",)),
        name="moe_combine",
    )(dest3, x2d, gate, ys, g.reshape(1, D))


def _moe_layer(x2d, norm_g, wg, bg, we, be, w1, w3, w2, out_g, final_norm):
    N, D = x2d.shape
    BLK = EXPERT_BLOCK
    n_slots = 2 * N + N_EXPERTS * BLK
    h, e_t, gate_t, cnt = _router(x2d, norm_g, wg, bg, we, be)
    counts = cnt[:, 0].astype(jnp.int32)
    padded = ((counts + BLK - 1) // BLK) * BLK
    seg_end = jnp.cumsum(padded)
    seg_start = seg_end - padded
    n_blocks = n_slots // BLK
    block_row0 = jnp.arange(n_blocks, dtype=jnp.int32) * BLK
    block_expert = jnp.minimum(
        jnp.sum((seg_end[None, :] <= block_row0[:, None]).astype(jnp.int32), axis=1), N_EXPERTS - 1)
    n_used = (seg_end[-1:] // BLK).astype(jnp.int32)
    dest_t = _rank(e_t, seg_start)
    xs = _dispatch(h, dest_t, n_slots)
    ys = _experts(xs, block_expert, n_used, w1, w3, w2)
    return _combine(x2d, dest_t, gate_t, ys, out_g, final_norm)


def _qkv_kernel(x_ref, g_ref, wq_ref, wk_ref, wv_ref, q_ref, k_ref, v_ref):
    h = _rms(x_ref[...], g_ref[...], RMS_EPS).astype(BF16)
    q = jnp.dot(h, wq_ref[...], preferred_element_type=F32)
    q_ref[...] = (q * (HEAD_DIM ** -0.5)).astype(BF16)
    k_ref[...] = jnp.dot(h, wk_ref[...], preferred_element_type=F32).astype(BF16)
    v_ref[...] = jnp.dot(h, wv_ref[...], preferred_element_type=F32).astype(BF16)


def _qkv(x2d, g, wq, wk, wv):
    N, D = x2d.shape
    DA = wq.shape[1]
    TM = ROW_TILE
    wspec = pl.BlockSpec((D, DA), lambda i: (0, 0))
    ospec = pl.BlockSpec((TM, DA), lambda i: (i, 0))
    osh = jax.ShapeDtypeStruct((N, DA), BF16)
    return pl.pallas_call(
        _qkv_kernel,
        out_shape=(osh, osh, osh),
        grid=(N // TM,),
        in_specs=[pl.BlockSpec((TM, D), lambda i: (i, 0)), pl.BlockSpec((1, D), lambda i: (0, 0)),
                  wspec, wspec, wspec],
        out_specs=(ospec, ospec, ospec),
        compiler_params=_cparams(("arbitrary",)),
        name="attn_qkv",
    )(x2d, g.reshape(1, D), wq.astype(BF16), wk.astype(BF16), wv.astype(BF16))


def _attn_kernel(slope_ref, q_ref, k_ref, v_ref, lam_ref, sg_ref, o_ref, *, lam_init):
    TQ = q_ref.shape[0]
    TK = ATTN_TK
    hd = pl.program_id(1)
    qi = pl.program_id(2)
    slope = slope_ref[hd]
    lamv = lam_ref[...]
    lam = (jnp.exp(jnp.sum(lamv[0:1] * lamv[1:2], axis=-1, keepdims=True))
           - jnp.exp(jnp.sum(lamv[2:3] * lamv[3:4], axis=-1, keepdims=True)) + lam_init)

    q = q_ref[...]
    lane = lax.broadcasted_iota(jnp.int32, q.shape, 1)
    zero = jnp.zeros_like(q)
    qz = jnp.concatenate([jnp.where(lane < HEAD_DIM, q, zero), jnp.where(lane >= HEAD_DIM, q, zero)], axis=0)
    col_bias = slope * lax.broadcasted_iota(jnp.int32, (1, TK), 1).astype(F32)
    tile_step = slope * TK

    def scores(j):
        kt = k_ref[pl.ds(pl.multiple_of(j * TK, TK), TK), :]
        return lax.dot_general(qz, kt, (((1,), (1,)), ((), ())), preferred_element_type=F32)

    def update(j, s, carry):
        m, l, acc = carry
        base = tile_step * j.astype(F32)
        m_new = jnp.maximum(m, jnp.max(s, axis=-1, keepdims=True) + base)
        p = jnp.exp(s - (m_new - base))
        alpha = jnp.exp(m - m_new)
        vt = v_ref[pl.ds(pl.multiple_of(j * TK, TK), TK), :]
        acc = alpha * acc + jnp.dot(p.astype(BF16), vt, preferred_element_type=F32)
        l = alpha * l + jnp.sum(p, axis=-1, keepdims=True)
        return m_new, l, acc

    def past_tile(j, carry):
        return update(j, scores(j) + col_bias, carry)

    def diag_tile(j, carry):
        rows = lax.broadcasted_iota(jnp.int32, (2 * TQ, TK), 0)
        rows = jnp.where(rows >= TQ, rows - TQ, rows)
        kcol = lax.broadcasted_iota(jnp.int32, (2 * TQ, TK), 1)
        qrel = rows + (qi * TQ - j * TK)
        allowed = kcol < (lax.shift_right_arithmetic(qrel, CHUNK_SHIFT) + 1) * CHUNK
        rel = jnp.where(kcol <= qrel, kcol, 2 * qrel - kcol).astype(F32)
        s = jnp.where(allowed, scores(j) + slope * rel, NEG_INF)
        return update(j, s, carry)

    init = (jnp.full((2 * TQ, 1), -jnp.inf, F32), jnp.zeros((2 * TQ, 1), F32),
            jnp.zeros((2 * TQ, v_ref.shape[1]), F32))
    r = TQ // TK
    carry = lax.fori_loop(0, qi * r, past_tile, init)
    carry = lax.fori_loop(qi * r, (qi + 1) * r, diag_tile, carry)
    _, l, acc = carry
    on = acc / l
    o = on[0:TQ] - lam * on[TQ:2 * TQ]
    o = _rms(o, sg_ref[...], SUBLN_EPS) * (1.0 - lam_init)
    o_ref[...] = o.astype(o_ref.dtype)


def _attention(q, k, v, lamv, subln_g, lam_init):
    B, S, DA = q.shape
    HW = 2 * HEAD_DIM
    TQ = ATTN_TQ
    slopes = jnp.array([2.0 ** (-8.0 * (h + 1) / N_HEADS) for h in range(N_HEADS)], F32)
    return pl.pallas_call(
        functools.partial(_attn_kernel, lam_init=lam_init),
        out_shape=jax.ShapeDtypeStruct((B, S, DA), BF16),
        grid_spec=pltpu.PrefetchScalarGridSpec(
            num_scalar_prefetch=1,
            grid=(B, N_HEADS, S // TQ),
            in_specs=[
                pl.BlockSpec((None, TQ, HW), lambda b, h, i, sl: (b, i, h)),
                pl.BlockSpec((None, S, HW), lambda b, h, i, sl: (b, 0, h)),
                pl.BlockSpec((None, S, HW), lambda b, h, i, sl: (b, 0, h)),
                pl.BlockSpec((4, HEAD_DIM), lambda b, h, i, sl: (0, 0)),
                pl.BlockSpec((1, HW), lambda b, h, i, sl: (0, 0)),
            ],
            out_specs=pl.BlockSpec((None, TQ, HW), lambda b, h, i, sl: (b, i, h)),
        ),
        compiler_params=_cparams(("arbitrary", "arbitrary", "arbitrary")),
        name="diff_attention",
    )(slopes, q, k, v, lamv, subln_g.reshape(1, HW))


def _wo_kernel(o_ref, w_ref, x_ref, out_ref):
    out_ref[...] = x_ref[...] + jnp.dot(o_ref[...], w_ref[...], preferred_element_type=F32)


def _wo(o2d, wo, x2d):
    N, D = x2d.shape
    DA = o2d.shape[1]
    TM = ROW_TILE
    return pl.pallas_call(
        _wo_kernel,
        out_shape=jax.ShapeDtypeStruct((N, D), F32),
        grid=(N // TM,),
        in_specs=[pl.BlockSpec((TM, DA), lambda i: (i, 0)), pl.BlockSpec((DA, D), lambda i: (0, 0)),
                  pl.BlockSpec((TM, D), lambda i: (i, 0))],
        out_specs=pl.BlockSpec((TM, D), lambda i: (i, 0)),
        compiler_params=_cparams(("arbitrary",)),
        name="attn_wo",
    )(o2d, wo.astype(BF16), x2d)


def _attn_layer(x, g, wq, wk, wv, wo, lq1, lk1, lq2, lk2, subln_g, layer_idx):
    B, S, D = x.shape
    x2d = x.reshape(B * S, D)
    q, k, v = _qkv(x2d, g, wq, wk, wv)
    DA = q.shape[1]
    lam_init = 0.8 - 0.6 * math.exp(-0.3 * layer_idx)
    lamv = jnp.stack([lq1, lk1, lq2, lk2]).astype(F32)
    o = _attention(q.reshape(B, S, DA), k.reshape(B, S, DA), v.reshape(B, S, DA), lamv, subln_g, lam_init)
    return _wo(o.reshape(B * S, DA), wo, x2d).reshape(B, S, D)


def kernel(x, norm_mix_g, norm_ffn_g, pool_w, pool_scale, attn_wq, attn_wk, attn_wv, attn_wo,
           attn_lq1, attn_lk1, attn_lq2, attn_lk2, attn_subln_g, moe_wg, moe_bg, moe_we, moe_be,
           moe_w1, moe_w3, moe_w2, final_g):
    B, S, D = x.shape
    depth = norm_mix_g.shape[0]
    for i in range(depth):
        j = i // 2
        if i % 2 == 0:
            x = _pool_layer(x, norm_mix_g[i], pool_w[j], pool_scale[j])
        else:
            x = _attn_layer(x, norm_mix_g[i], attn_wq[j], attn_wk[j], attn_wv[j], attn_wo[j],
                            attn_lq1[j], attn_lk1[j], attn_lq2[j], attn_lk2[j], attn_subln_g[j], i)
        last = i == depth - 1
        x = _moe_layer(x.reshape(B * S, D), norm_ffn_g[i], moe_wg[i], moe_bg[i], moe_we[i], moe_be[i],
                       moe_w1[i], moe_w3[i], moe_w2[i], final_g, last).reshape(B, S, D)
    return x
```

```python
import functools
import math

import jax
import jax.numpy as jnp
from jax import lax
from jax.experimental import pallas as pl
from jax.experimental.pallas import tpu as pltpu

F32 = jnp.float32
BF16 = jnp.bfloat16

RMS_EPS = 1e-6
SUBLN_EPS = 1e-5
POOL_WINDOWS = (2, 4, 8, 16)
CHUNK = 64
N_HEADS = 8
HEAD_DIM = 64
N_GROUPS = 4
EXPERTS_PER_GROUP = 8
N_EXPERTS = N_GROUPS * EXPERTS_PER_GROUP
NEG_INF = -1e30
LOG2E = math.log2(math.e)

VMEM_LIMIT_BYTES = 48 * 1024 * 1024
POOL_TILE = 256
ROW_TILE = 512
EXPERT_BLOCK = 256
ATTN_TQ = 512
ATTN_TK = 512
LOGIT_ROWS = 40


def _cparams(sem):
    return pltpu.CompilerParams(dimension_semantics=sem, vmem_limit_bytes=VMEM_LIMIT_BYTES)


def _rms(x, g, eps):
    return x * lax.rsqrt(jnp.mean(x * x, axis=-1, keepdims=True) + eps) * g


def _pool_kernel(x_ref, g_ref, band_ref, icnt_ref, w_ref, sc_ref, o_ref, hh_ref):
    T = POOL_TILE
    si = pl.program_id(1)

    @pl.when(si == 0)
    def _():
        hh_ref[0:T, :] = jnp.zeros((T, hh_ref.shape[1]), BF16)

    @pl.when(si > 0)
    def _():
        hh_ref[0:T, :] = hh_ref[T:2 * T, :]

    x = x_ref[...]
    h = _rms(x, g_ref[...], RMS_EPS)
    hh_ref[T:2 * T, :] = h.astype(BF16)
    gw = w_ref.shape[1]
    outs = []
    for gi in range(len(POOL_WINDOWS)):
        cols = slice(gi * gw, (gi + 1) * gw)
        wsum = jnp.dot(band_ref[gi], hh_ref[:, cols], preferred_element_type=F32)
        mix = wsum * icnt_ref[gi] - h[:, cols]
        outs.append(jnp.dot(mix.astype(BF16), w_ref[gi], preferred_element_type=F32))
    o_ref[...] = x + jnp.concatenate(outs, axis=-1) * sc_ref[...]


def _pool_layer(x, g, w, scale):
    B, S, D = x.shape
    T = POOL_TILE
    G = len(POOL_WINDOWS)
    r = jnp.arange(T)[:, None]
    c = jnp.arange(2 * T)[None, :]
    band = jnp.stack([((c <= r + T) & (c > r + T - win)) for win in POOL_WINDOWS]).astype(BF16)
    pos = jnp.arange(S, dtype=jnp.int32)
    icnt = jnp.stack([1.0 / jnp.minimum(pos + 1, win).astype(F32) for win in POOL_WINDOWS])
    icnt = icnt.reshape(G, S, 1)
    return pl.pallas_call(
        _pool_kernel,
        out_shape=jax.ShapeDtypeStruct((B, S, D), F32),
        grid=(B, S // T),
        in_specs=[
            pl.BlockSpec((None, T, D), lambda b, s: (b, s, 0)),
            pl.BlockSpec((1, D), lambda b, s: (0, 0)),
            pl.BlockSpec((G, T, 2 * T), lambda b, s: (0, 0, 0)),
            pl.BlockSpec((G, T, 1), lambda b, s: (0, s, 0)),
            pl.BlockSpec((G, D // G, D // G), lambda b, s: (0, 0, 0)),
            pl.BlockSpec((1, D), lambda b, s: (0, 0)),
        ],
        out_specs=pl.BlockSpec((None, T, D), lambda b, s: (b, s, 0)),
        scratch_shapes=[pltpu.VMEM((2 * T, D), BF16)],
        compiler_params=_cparams(("arbitrary", "arbitrary")),
        name="pool_mixer",
    )(x, g.reshape(1, D), band, icnt, w.astype(BF16), scale.reshape(1, D))


def _router_kernel(x_ref, g_ref, wt_ref, b_ref, h_ref, e_ref, gate_ref, cnt_ref):
    i = pl.program_id(0)
    x = x_ref[...]
    h = _rms(x, g_ref[...], RMS_EPS)
    h_ref[...] = h
    lt = lax.dot_general(wt_ref[...], h, (((1,), (1,)), ((), ())),
                         precision=lax.Precision.HIGHEST, preferred_element_type=F32)
    lt = lt + b_ref[...]
    g0, g1, g2, g3 = lt[0:1], lt[1:2], lt[2:3], lt[3:4]
    gmax = jnp.maximum(jnp.maximum(g0, g1), jnp.maximum(g2, g3))
    gidx = jnp.where(g0 == gmax, 0, jnp.where(g1 == gmax, 1, jnp.where(g2 == gmax, 2, 3)))
    gsum = jnp.exp(g0 - gmax) + jnp.exp(g1 - gmax) + jnp.exp(g2 - gmax) + jnp.exp(g3 - gmax)
    g_p = 1.0 / gsum
    E = EXPERTS_PER_GROUP
    e_in = jnp.where(gidx == 0, lt[8:8 + E],
                     jnp.where(gidx == 1, lt[8 + E:8 + 2 * E],
                               jnp.where(gidx == 2, lt[8 + 2 * E:8 + 3 * E], lt[8 + 3 * E:8 + 4 * E])))
    io = lax.broadcasted_iota(jnp.int32, e_in.shape, 0)
    v0 = jnp.max(e_in, axis=0, keepdims=True)
    i0 = jnp.min(jnp.where(e_in == v0, io, E), axis=0, keepdims=True)
    rest = jnp.where(io == i0, -jnp.inf, e_in)
    v1 = jnp.max(rest, axis=0, keepdims=True)
    i1 = jnp.min(jnp.where(rest == v1, io, E), axis=0, keepdims=True)
    ex = jnp.exp(v1 - v0)
    p0 = 1.0 / (1.0 + ex)
    e0 = gidx * E + i0
    e1 = gidx * E + i1
    e_ref[0:1, :] = e0
    e_ref[1:2, :] = e1
    gate_ref[0:1, :] = g_p * p0
    gate_ref[1:2, :] = g_p * (ex * p0)
    io32 = lax.broadcasted_iota(jnp.int32, (N_EXPERTS, x.shape[0]), 0)
    oh = (io32 == e0).astype(F32) + (io32 == e1).astype(F32)
    tile_cnt = jnp.sum(oh, axis=1, keepdims=True)

    @pl.when(i == 0)
    def _():
        cnt_ref[...] = jnp.zeros(cnt_ref.shape, F32)

    cnt_ref[...] += jnp.broadcast_to(tile_cnt, cnt_ref.shape)


def _router(x2d, g, wg, bg, we, be):
    N, D = x2d.shape
    TM = ROW_TILE
    wt = jnp.zeros((LOGIT_ROWS, D), F32).at[0:N_GROUPS].set(wg.T).at[8:8 + N_EXPERTS].set(we.T)
    bt = jnp.zeros((LOGIT_ROWS, 1), F32).at[0:N_GROUPS, 0].set(bg).at[8:8 + N_EXPERTS, 0].set(be)
    return pl.pallas_call(
        _router_kernel,
        out_shape=(
            jax.ShapeDtypeStruct((N, D), F32),
            jax.ShapeDtypeStruct((2, N), jnp.int32),
            jax.ShapeDtypeStruct((2, N), F32),
            jax.ShapeDtypeStruct((N_EXPERTS, 128), F32),
        ),
        grid=(N // TM,),
        in_specs=[
            pl.BlockSpec((TM, D), lambda i: (i, 0)),
            pl.BlockSpec((1, D), lambda i: (0, 0)),
            pl.BlockSpec((LOGIT_ROWS, D), lambda i: (0, 0)),
            pl.BlockSpec((LOGIT_ROWS, 1), lambda i: (0, 0)),
        ],
        out_specs=(
            pl.BlockSpec((TM, D), lambda i: (i, 0)),
            pl.BlockSpec((2, TM), lambda i: (0, i)),
            pl.BlockSpec((2, TM), lambda i: (0, i)),
            pl.BlockSpec((N_EXPERTS, 128), lambda i: (0, 0)),
        ),
        compiler_params=_cparams(("arbitrary",)),
        name="moe_router",
    )(x2d, g.reshape(1, D), wt, bt)


def _rank_kernel(e_ref, start_ref, tri_ref, dest_ref, carry_ref):
    i = pl.program_id(0)

    @pl.when(i == 0)
    def _():
        carry_ref[...] = jnp.zeros(carry_ref.shape, F32)

    TM = e_ref.shape[1]
    io32 = lax.broadcasted_iota(jnp.int32, (N_EXPERTS, TM), 0)
    base = carry_ref[...] + start_ref[...]
    for k in range(2):
        oh = io32 == e_ref[k:k + 1, :]
        ohf = oh.astype(F32)
        before = jnp.dot(ohf.astype(BF16), tri_ref[...], preferred_element_type=F32)
        slot = jnp.sum(jnp.where(oh, before + base, 0.0), axis=0, keepdims=True)
        dest_ref[k:k + 1, :] = slot.astype(jnp.int32)
        base = base + jnp.sum(ohf, axis=1, keepdims=True)
    carry_ref[...] = base - start_ref[...]


def _rank(e_t, seg_start):
    _, N = e_t.shape
    TM = ROW_TILE
    tri = (jnp.arange(TM)[:, None] < jnp.arange(TM)[None, :]).astype(BF16)
    return pl.pallas_call(
        _rank_kernel,
        out_shape=jax.ShapeDtypeStruct((2, N), jnp.int32),
        grid=(N // TM,),
        in_specs=[
            pl.BlockSpec((2, TM), lambda i: (0, i)),
            pl.BlockSpec((N_EXPERTS, 1), lambda i: (0, 0)),
            pl.BlockSpec((TM, TM), lambda i: (0, 0)),
        ],
        out_specs=pl.BlockSpec((2, TM), lambda i: (0, i)),
        scratch_shapes=[pltpu.VMEM((N_EXPERTS, 1), F32)],
        compiler_params=_cparams(("arbitrary",)),
        name="moe_rank",
    )(e_t, seg_start.astype(F32).reshape(N_EXPERTS, 1), tri)


def _row_copy(src_ref, src_row, dst_ref, dst_row, sem):
    return pltpu.make_async_copy(src_ref.at[pl.ds(src_row, 1)], dst_ref.at[pl.ds(dst_row, 1)], sem)


def _dispatch_kernel(dest_ref, h_ref, xs_in_ref, xs_ref, sem):
    del xs_in_ref
    TM = h_ref.shape[0]

    def issue(t, carry):
        _row_copy(h_ref, t, xs_ref, dest_ref[0, 0, t], sem).start()
        _row_copy(h_ref, t, xs_ref, dest_ref[0, 1, t], sem).start()
        return carry

    lax.fori_loop(0, TM, issue, 0, unroll=8)
    for _ in range(2):
        pltpu.make_async_copy(h_ref, xs_ref.at[pl.ds(0, TM)], sem).wait()


def _dispatch(h, dest_t, n_slots):
    N, D = h.shape
    TM = ROW_TILE
    dest3 = dest_t.reshape(2, N // TM, TM).transpose(1, 0, 2)
    xs0 = jnp.zeros((n_slots, D), h.dtype)
    return pl.pallas_call(
        _dispatch_kernel,
        out_shape=jax.ShapeDtypeStruct((n_slots, D), h.dtype),
        grid=(N // TM,),
        in_specs=[
            pl.BlockSpec((1, 2, TM), lambda i: (i, 0, 0), memory_space=pltpu.SMEM),
            pl.BlockSpec((TM, D), lambda i: (i, 0)),
            pl.BlockSpec(memory_space=pl.ANY),
        ],
        out_specs=pl.BlockSpec(memory_space=pl.ANY),
        scratch_shapes=[pltpu.SemaphoreType.DMA],
        input_output_aliases={2: 0},
        compiler_params=_cparams(("arbitrary",)),
        name="moe_dispatch",
    )(dest3, h, xs0)


def _expert_kernel(be_ref, nu_ref, xs_ref, w1_ref, w3_ref, w2_ref, ys_ref, w1b, w3b, w2b):
    i = pl.program_id(0)

    @pl.when(i < nu_ref[0])
    def _():
        prev = be_ref[jnp.maximum(i - 1, 0)]

        @pl.when((i == 0) | (be_ref[i] != prev))
        def _():
            w1b[...] = w1_ref[0].astype(BF16)
            w3b[...] = w3_ref[0].astype(BF16)
            w2b[...] = w2_ref[0].astype(BF16)

        xb = xs_ref[...].astype(BF16)
        a = jnp.dot(xb, w1b[...], preferred_element_type=F32)
        b = jnp.dot(xb, w3b[...], preferred_element_type=F32)
        act = a * (1.0 / (1.0 + jnp.exp(-a))) * b
        ys_ref[...] = jnp.dot(act.astype(BF16), w2b[...], preferred_element_type=F32)

    @pl.when(i >= nu_ref[0])
    def _():
        ys_ref[...] = jnp.zeros(ys_ref.shape, F32)


def _experts(xs, block_expert, n_used, w1, w3, w2):
    n_slots, D = xs.shape
    DE = w1.shape[2]
    BLK = EXPERT_BLOCK
    n_blocks = n_slots // BLK

    def xmap(i, be, nu):
        return (jnp.minimum(i, nu[0] - 1), 0)

    def wmap(i, be, nu):
        return (be[i], 0, 0)

    return pl.pallas_call(
        _expert_kernel,
        out_shape=jax.ShapeDtypeStruct((n_slots, D), F32),
        grid_spec=pltpu.PrefetchScalarGridSpec(
            num_scalar_prefetch=2,
            grid=(n_blocks,),
            in_specs=[
                pl.BlockSpec((BLK, D), xmap),
                pl.BlockSpec((1, D, DE), wmap),
                pl.BlockSpec((1, D, DE), wmap),
                pl.BlockSpec((1, DE, D), wmap),
            ],
            out_specs=pl.BlockSpec((BLK, D), lambda i, be, nu: (i, 0)),
            scratch_shapes=[
                pltpu.VMEM((D, DE), BF16),
                pltpu.VMEM((D, DE), BF16),
                pltpu.VMEM((DE, D), BF16),
            ],
        ),
        compiler_params=_cparams(("arbitrary",)),
        name="moe_experts",
    )(block_expert, n_used, xs, w1, w3, w2)


def _combine_kernel(dest_ref, x_ref, gate_ref, ys_ref, g_ref, o_ref, buf0, buf1, sem, *, final_norm):
    TM = x_ref.shape[0]

    def issue(t, carry):
        _row_copy(ys_ref, dest_ref[0, 0, t], buf0, t, sem).start()
        _row_copy(ys_ref, dest_ref[0, 1, t], buf1, t, sem).start()
        return carry

    lax.fori_loop(0, TM, issue, 0, unroll=8)
    pltpu.make_async_copy(ys_ref.at[pl.ds(0, TM)], buf0, sem).wait()
    pltpu.make_async_copy(ys_ref.at[pl.ds(0, TM)], buf1, sem).wait()
    gate = gate_ref[...]
    y = x_ref[...] + gate[:, 0:1] * buf0[...] + gate[:, 1:2] * buf1[...]
    if final_norm:
        y = _rms(y, g_ref[...], RMS_EPS)
    o_ref[...] = y


def _combine(x2d, dest_t, gate_t, ys, g, final_norm):
    N, D = x2d.shape
    TM = ROW_TILE
    dest3 = dest_t.reshape(2, N // TM, TM).transpose(1, 0, 2)
    gate = gate_t.T
    return pl.pallas_call(
        functools.partial(_combine_kernel, final_norm=final_norm),
        out_shape=jax.ShapeDtypeStruct((N, D), F32),
        grid=(N // TM,),
        in_specs=[
            pl.BlockSpec((1, 2, TM), lambda i: (i, 0, 0), memory_space=pltpu.SMEM),
            pl.BlockSpec((TM, D), lambda i: (i, 0)),
            pl.BlockSpec((TM, 2), lambda i: (i, 0)),
            pl.BlockSpec(memory_space=pl.ANY),
            pl.BlockSpec((1, D), lambda i: (0, 0)),
        ],
        out_specs=pl.BlockSpec((TM, D), lambda i: (i, 0)),
        scratch_shapes=[pltpu.VMEM((TM, D), F32), pltpu.VMEM((TM, D), F32), pltpu.SemaphoreType.DMA],
        compiler_params=_cparams(("arbitrary",)),
        name="moe_combine",
    )(dest3, x2d, gate, ys, g.reshape(1, D))


def _moe_layer(x2d, norm_g, wg, bg, we, be, w1, w3, w2, out_g, final_norm):
    N, D = x2d.shape
    BLK = EXPERT_BLOCK
    n_slots = 2 * N + N_EXPERTS * BLK
    h, e_t, gate_t, cnt = _router(x2d, norm_g, wg, bg, we, be)
    counts = cnt[:, 0].astype(jnp.int32)
    padded = ((counts + BLK - 1) // BLK) * BLK
    seg_end = jnp.cumsum(padded)
    seg_start = seg_end - padded
    n_blocks = n_slots // BLK
    block_row0 = jnp.arange(n_blocks, dtype=jnp.int32) * BLK
    block_expert = jnp.minimum(
        jnp.sum((seg_end[None, :] <= block_row0[:, None]).astype(jnp.int32), axis=1), N_EXPERTS - 1)
    n_used = (seg_end[-1:] // BLK).astype(jnp.int32)
    dest_t = _rank(e_t, seg_start)
    xs = _dispatch(h, dest_t, n_slots)
    ys = _experts(xs, block_expert, n_used, w1, w3, w2)
    return _combine(x2d, dest_t, gate_t, ys, out_g, final_norm)


def _qkv_kernel(x_ref, g_ref, wq_ref, wk_ref, wv_ref, q_ref, k_ref, v_ref):
    h = _rms(x_ref[...], g_ref[...], RMS_EPS).astype(BF16)
    q = jnp.dot(h, wq_ref[...], preferred_element_type=F32)
    q_ref[...] = (q * (LOG2E * HEAD_DIM ** -0.5)).astype(BF16)
    k_ref[...] = jnp.dot(h, wk_ref[...], preferred_element_type=F32).astype(BF16)
    v_ref[...] = jnp.dot(h, wv_ref[...], preferred_element_type=F32).astype(BF16)


def _qkv(x2d, g, wq, wk, wv):
    N, D = x2d.shape
    DA = wq.shape[1]
    TM = ROW_TILE
    wspec = pl.BlockSpec((D, DA), lambda i: (0, 0))
    ospec = pl.BlockSpec((TM, DA), lambda i: (i, 0))
    osh = jax.ShapeDtypeStruct((N, DA), BF16)
    return pl.pallas_call(
        _qkv_kernel,
        out_shape=(osh, osh, osh),
        grid=(N // TM,),
        in_specs=[pl.BlockSpec((TM, D), lambda i: (i, 0)), pl.BlockSpec((1, D), lambda i: (0, 0)),
                  wspec, wspec, wspec],
        out_specs=(ospec, ospec, ospec),
        compiler_params=_cparams(("arbitrary",)),
        name="attn_qkv",
    )(x2d, g.reshape(1, D), wq.astype(BF16), wk.astype(BF16), wv.astype(BF16))


def _attn_kernel(slope_ref, q_ref, k_ref, v_ref, db_ref, lam_ref, sg_ref, o_ref,
                 qz_ref, s_ref, p_ref, m_ref, l_ref, acc_ref, *, lam_init):
    TQ = q_ref.shape[0]
    TK = ATTN_TK
    hd = pl.program_id(1)
    qi = pl.program_id(2)
    slope = slope_ref[hd]

    q = q_ref[...]
    lane = lax.broadcasted_iota(jnp.int32, q.shape, 1)
    zero = jnp.zeros_like(q)
    qz_ref[0:TQ, :] = jnp.where(lane < HEAD_DIM, q, zero)
    qz_ref[TQ:2 * TQ, :] = jnp.where(lane >= HEAD_DIM, q, zero)
    m_ref[...] = jnp.full(m_ref.shape, -jnp.inf, F32)
    l_ref[...] = jnp.zeros(l_ref.shape, F32)
    acc_ref[...] = jnp.zeros(acc_ref.shape, F32)
    LW = acc_ref.shape[1]
    NC = TK // LW
    kpos = lax.broadcasted_iota(jnp.int32, (1, TK), 1)
    q0 = qi * TQ

    def tile(j, diagonal):
        koff = pl.multiple_of(j * TK, TK)
        s = lax.dot_general(qz_ref[...], k_ref[pl.ds(koff, TK), :], (((1,), (1,)), ((), ())),
                            preferred_element_type=F32)
        if diagonal:
            s = jnp.concatenate([s[0:TQ] + db_ref[...], s[TQ:2 * TQ] + db_ref[...]], axis=0)
        else:
            s = s + slope * (kpos + (j * TK - q0)).astype(F32)
        s_ref[...] = s
        mx = s[:, 0:LW]
        for c in range(1, NC):
            mx = jnp.maximum(mx, s[:, c * LW:(c + 1) * LW])
        m_old = m_ref[...]
        m_new = jnp.maximum(m_old, jnp.max(mx, axis=-1, keepdims=True))
        alpha = jnp.exp2(m_old - m_new)
        m_ref[...] = m_new
        psum = None
        for c in range(NC):
            p = jnp.exp2(s_ref[:, c * LW:(c + 1) * LW] - m_new)
            p_ref[:, c * LW:(c + 1) * LW] = p.astype(BF16)
            psum = p if psum is None else psum + p
        l_ref[...] = alpha * l_ref[...] + psum
        acc_ref[...] = alpha * acc_ref[...] + jnp.dot(p_ref[...], v_ref[pl.ds(koff, TK), :],
                                                      preferred_element_type=F32)

    def past_tile(j, carry):
        tile(j, False)
        return carry

    lax.fori_loop(0, qi, past_tile, 0)
    tile(qi, True)

    lamv = lam_ref[...]
    lam = (jnp.exp(jnp.sum(lamv[0:1] * lamv[1:2], axis=-1, keepdims=True))
           - jnp.exp(jnp.sum(lamv[2:3] * lamv[3:4], axis=-1, keepdims=True)) + lam_init)
    on = acc_ref[...] / jnp.sum(l_ref[...], axis=-1, keepdims=True)
    o = on[0:TQ] - lam * on[TQ:2 * TQ]
    o = _rms(o, sg_ref[...], SUBLN_EPS) * (1.0 - lam_init)
    o_ref[...] = o.astype(o_ref.dtype)


def _attention(q, k, v, lamv, subln_g, lam_init):
    B, S, DA = q.shape
    HW = 2 * HEAD_DIM
    TQ = ATTN_TQ
    TK = ATTN_TK
    assert TK == TQ and S % TK == 0 and TK % CHUNK == 0
    slopes = jnp.array([LOG2E * 2.0 ** (-8.0 * (h + 1) / N_HEADS) for h in range(N_HEADS)], F32)
    qr = jnp.arange(TQ, dtype=jnp.int32)[:, None]
    kc = jnp.arange(TK, dtype=jnp.int32)[None, :]
    rel = jnp.where(kc <= qr, kc, 2 * qr - kc).astype(F32)
    diag_bias = jnp.where((kc // CHUNK <= qr // CHUNK)[None], slopes[:, None, None] * rel[None], NEG_INF)
    return pl.pallas_call(
        functools.partial(_attn_kernel, lam_init=lam_init),
        out_shape=jax.ShapeDtypeStruct((B, S, DA), BF16),
        grid_spec=pltpu.PrefetchScalarGridSpec(
            num_scalar_prefetch=1,
            grid=(B, N_HEADS, S // TQ),
            in_specs=[
                pl.BlockSpec((None, TQ, HW), lambda b, h, i, sl: (b, i, h)),
                pl.BlockSpec((None, S, HW), lambda b, h, i, sl: (b, 0, h)),
                pl.BlockSpec((None, S, HW), lambda b, h, i, sl: (b, 0, h)),
                pl.BlockSpec((None, TQ, TK), lambda b, h, i, sl: (h, 0, 0)),
                pl.BlockSpec((4, HEAD_DIM), lambda b, h, i, sl: (0, 0)),
                pl.BlockSpec((1, HW), lambda b, h, i, sl: (0, 0)),
            ],
            out_specs=pl.BlockSpec((None, TQ, HW), lambda b, h, i, sl: (b, i, h)),
            scratch_shapes=[
                pltpu.VMEM((2 * TQ, HW), BF16),
                pltpu.VMEM((2 * TQ, TK), F32),
                pltpu.VMEM((2 * TQ, TK), BF16),
                pltpu.VMEM((2 * TQ, HW), F32),
                pltpu.VMEM((2 * TQ, HW), F32),
                pltpu.VMEM((2 * TQ, HW), F32),
            ],
        ),
        compiler_params=_cparams(("arbitrary", "arbitrary", "arbitrary")),
        name="diff_attention",
    )(slopes, q, k, v, diag_bias, lamv, subln_g.reshape(1, HW))


def _wo_kernel(o_ref, w_ref, x_ref, out_ref):
    out_ref[...] = x_ref[...] + jnp.dot(o_ref[...], w_ref[...], preferred_element_type=F32)


def _wo(o2d, wo, x2d):
    N, D = x2d.shape
    DA = o2d.shape[1]
    TM = ROW_TILE
    return pl.pallas_call(
        _wo_kernel,
        out_shape=jax.ShapeDtypeStruct((N, D), F32),
        grid=(N // TM,),
        in_specs=[pl.BlockSpec((TM, DA), lambda i: (i, 0)), pl.BlockSpec((DA, D), lambda i: (0, 0)),
                  pl.BlockSpec((TM, D), lambda i: (i, 0))],
        out_specs=pl.BlockSpec((TM, D), lambda i: (i, 0)),
        compiler_params=_cparams(("arbitrary",)),
        name="attn_wo",
    )(o2d, wo.astype(BF16), x2d)


def _attn_layer(x, g, wq, wk, wv, wo, lq1, lk1, lq2, lk2, subln_g, layer_idx):
    B, S, D = x.shape
    x2d = x.reshape(B * S, D)
    q, k, v = _qkv(x2d, g, wq, wk, wv)
    DA = q.shape[1]
    lam_init = 0.8 - 0.6 * math.exp(-0.3 * layer_idx)
    lamv = jnp.stack([lq1, lk1, lq2, lk2]).astype(F32)
    o = _attention(q.reshape(B, S, DA), k.reshape(B, S, DA), v.reshape(B, S, DA), lamv, subln_g, lam_init)
    return _wo(o.reshape(B * S, DA), wo, x2d).reshape(B, S, D)


def kernel(x, norm_mix_g, norm_ffn_g, pool_w, pool_scale, attn_wq, attn_wk, attn_wv, attn_wo,
           attn_lq1, attn_lk1, attn_lq2, attn_lk2, attn_subln_g, moe_wg, moe_bg, moe_we, moe_be,
           moe_w1, moe_w3, moe_w2, final_g):
    B, S, D = x.shape
    depth = norm_mix_g.shape[0]
    for i in range(depth):
        j = i // 2
        if i % 2 == 0:
            x = _pool_layer(x, norm_mix_g[i], pool_w[j], pool_scale[j])
        else:
            x = _attn_layer(x, norm_mix_g[i], attn_wq[j], attn_wk[j], attn_wv[j], attn_wo[j],
                            attn_lq1[j], attn_lk1[j], attn_lq2[j], attn_lk2[j], attn_subln_g[j], i)
        last = i == depth - 1
        x = _moe_layer(x.reshape(B * S, D), norm_ffn_g[i], moe_wg[i], moe_bg[i], moe_we[i], moe_be[i],
                       moe_w1[i], moe_w3[i], moe_w2[i], final_g, last).reshape(B, S, D)
    return x
```

```python
import functools
import math

import jax
import jax.numpy as jnp
from jax import lax
from jax.experimental import pallas as pl
from jax.experimental.pallas import tpu as pltpu

F32 = jnp.float32
BF16 = jnp.bfloat16

RMS_EPS = 1e-6
SUBLN_EPS = 1e-5
POOL_WINDOWS = (2, 4, 8, 16)
CHUNK = 64
N_HEADS = 8
HEAD_DIM = 64
N_GROUPS = 4
EXPERTS_PER_GROUP = 8
N_EXPERTS = N_GROUPS * EXPERTS_PER_GROUP
NEG_INF = -1e30
LOG2E = math.log2(math.e)

VMEM_LIMIT_BYTES = 48 * 1024 * 1024
POOL_TILE = 256
ROW_TILE = 512
EXPERT_BLOCK = 512
ATTN_TQ = 512
ATTN_TK = 512
LANES = 128
LOGIT_ROWS = 40


def _cparams(sem):
    return pltpu.CompilerParams(dimension_semantics=sem, vmem_limit_bytes=VMEM_LIMIT_BYTES)


def _rms(x, g, eps):
    return x * lax.rsqrt(jnp.mean(x * x, axis=-1, keepdims=True) + eps) * g


def _pool_kernel(x_ref, g_ref, band_ref, icnt_ref, w_ref, sc_ref, o_ref, hh_ref):
    T = POOL_TILE
    si = pl.program_id(1)

    @pl.when(si == 0)
    def _():
        hh_ref[0:T, :] = jnp.zeros((T, hh_ref.shape[1]), BF16)

    @pl.when(si > 0)
    def _():
        hh_ref[0:T, :] = hh_ref[T:2 * T, :]

    x = x_ref[...]
    h = _rms(x, g_ref[...], RMS_EPS)
    hh_ref[T:2 * T, :] = h.astype(BF16)
    gw = w_ref.shape[1]
    outs = []
    for gi in range(len(POOL_WINDOWS)):
        cols = slice(gi * gw, (gi + 1) * gw)
        wsum = jnp.dot(band_ref[gi], hh_ref[:, cols], preferred_element_type=F32)
        mix = wsum * icnt_ref[gi] - h[:, cols]
        outs.append(jnp.dot(mix.astype(BF16), w_ref[gi], preferred_element_type=F32))
    o_ref[...] = x + jnp.concatenate(outs, axis=-1) * sc_ref[...]


def _pool_layer(x, g, w, scale):
    B, S, D = x.shape
    T = POOL_TILE
    G = len(POOL_WINDOWS)
    r = jnp.arange(T)[:, None]
    c = jnp.arange(2 * T)[None, :]
    band = jnp.stack([((c <= r + T) & (c > r + T - win)) for win in POOL_WINDOWS]).astype(BF16)
    pos = jnp.arange(S, dtype=jnp.int32)
    icnt = jnp.stack([1.0 / jnp.minimum(pos + 1, win).astype(F32) for win in POOL_WINDOWS])
    icnt = icnt.reshape(G, S, 1)
    return pl.pallas_call(
        _pool_kernel,
        out_shape=jax.ShapeDtypeStruct((B, S, D), F32),
        grid=(B, S // T),
        in_specs=[
            pl.BlockSpec((None, T, D), lambda b, s: (b, s, 0)),
            pl.BlockSpec((1, D), lambda b, s: (0, 0)),
            pl.BlockSpec((G, T, 2 * T), lambda b, s: (0, 0, 0)),
            pl.BlockSpec((G, T, 1), lambda b, s: (0, s, 0)),
            pl.BlockSpec((G, D // G, D // G), lambda b, s: (0, 0, 0)),
            pl.BlockSpec((1, D), lambda b, s: (0, 0)),
        ],
        out_specs=pl.BlockSpec((None, T, D), lambda b, s: (b, s, 0)),
        scratch_shapes=[pltpu.VMEM((2 * T, D), BF16)],
        compiler_params=_cparams(("arbitrary", "arbitrary")),
        name="pool_mixer",
    )(x, g.reshape(1, D), band, icnt, w.astype(BF16), scale.reshape(1, D))


def _router_kernel(x_ref, g_ref, wt_ref, b_ref, h_ref, e_ref, gate_ref, cnt_ref):
    i = pl.program_id(0)
    x = x_ref[...]
    h = _rms(x, g_ref[...], RMS_EPS)
    h_ref[...] = h
    lt = lax.dot_general(wt_ref[...], h, (((1,), (1,)), ((), ())),
                         precision=lax.Precision.HIGHEST, preferred_element_type=F32)
    lt = lt + b_ref[...]
    g0, g1, g2, g3 = lt[0:1], lt[1:2], lt[2:3], lt[3:4]
    gmax = jnp.maximum(jnp.maximum(g0, g1), jnp.maximum(g2, g3))
    gidx = jnp.where(g0 == gmax, 0, jnp.where(g1 == gmax, 1, jnp.where(g2 == gmax, 2, 3)))
    gsum = jnp.exp(g0 - gmax) + jnp.exp(g1 - gmax) + jnp.exp(g2 - gmax) + jnp.exp(g3 - gmax)
    g_p = 1.0 / gsum
    E = EXPERTS_PER_GROUP
    e_in = jnp.where(gidx == 0, lt[8:8 + E],
                     jnp.where(gidx == 1, lt[8 + E:8 + 2 * E],
                               jnp.where(gidx == 2, lt[8 + 2 * E:8 + 3 * E], lt[8 + 3 * E:8 + 4 * E])))
    io = lax.broadcasted_iota(jnp.int32, e_in.shape, 0)
    v0 = jnp.max(e_in, axis=0, keepdims=True)
    i0 = jnp.min(jnp.where(e_in == v0, io, E), axis=0, keepdims=True)
    rest = jnp.where(io == i0, -jnp.inf, e_in)
    v1 = jnp.max(rest, axis=0, keepdims=True)
    i1 = jnp.min(jnp.where(rest == v1, io, E), axis=0, keepdims=True)
    ex = jnp.exp(v1 - v0)
    p0 = 1.0 / (1.0 + ex)
    e0 = gidx * E + i0
    e1 = gidx * E + i1
    e_ref[0:1, :] = e0
    e_ref[1:2, :] = e1
    gate_ref[0:1, :] = g_p * p0
    gate_ref[1:2, :] = g_p * (ex * p0)
    io32 = lax.broadcasted_iota(jnp.int32, (N_EXPERTS, x.shape[0]), 0)
    oh = (io32 == e0).astype(F32) + (io32 == e1).astype(F32)
    tile_cnt = jnp.sum(oh, axis=1, keepdims=True)

    @pl.when(i == 0)
    def _():
        cnt_ref[...] = jnp.zeros(cnt_ref.shape, F32)

    cnt_ref[...] += jnp.broadcast_to(tile_cnt, cnt_ref.shape)


def _router(x2d, g, wg, bg, we, be):
    N, D = x2d.shape
    TM = ROW_TILE
    wt = jnp.zeros((LOGIT_ROWS, D), F32).at[0:N_GROUPS].set(wg.T).at[8:8 + N_EXPERTS].set(we.T)
    bt = jnp.zeros((LOGIT_ROWS, 1), F32).at[0:N_GROUPS, 0].set(bg).at[8:8 + N_EXPERTS, 0].set(be)
    return pl.pallas_call(
        _router_kernel,
        out_shape=(
            jax.ShapeDtypeStruct((N, D), F32),
            jax.ShapeDtypeStruct((2, N), jnp.int32),
            jax.ShapeDtypeStruct((2, N), F32),
            jax.ShapeDtypeStruct((N_EXPERTS, 128), F32),
        ),
        grid=(N // TM,),
        in_specs=[
            pl.BlockSpec((TM, D), lambda i: (i, 0)),
            pl.BlockSpec((1, D), lambda i: (0, 0)),
            pl.BlockSpec((LOGIT_ROWS, D), lambda i: (0, 0)),
            pl.BlockSpec((LOGIT_ROWS, 1), lambda i: (0, 0)),
        ],
        out_specs=(
            pl.BlockSpec((TM, D), lambda i: (i, 0)),
            pl.BlockSpec((2, TM), lambda i: (0, i)),
            pl.BlockSpec((2, TM), lambda i: (0, i)),
            pl.BlockSpec((N_EXPERTS, 128), lambda i: (0, 0)),
        ),
        compiler_params=_cparams(("arbitrary",)),
        name="moe_router",
    )(x2d, g.reshape(1, D), wt, bt)


def _rank_kernel(e_ref, start_ref, tri_ref, dest_ref, carry_ref):
    i = pl.program_id(0)

    @pl.when(i == 0)
    def _():
        carry_ref[...] = jnp.zeros(carry_ref.shape, F32)

    TM = e_ref.shape[1]
    io32 = lax.broadcasted_iota(jnp.int32, (N_EXPERTS, TM), 0)
    base = carry_ref[...] + start_ref[...]
    for k in range(2):
        oh = io32 == e_ref[k:k + 1, :]
        ohf = oh.astype(F32)
        before = jnp.dot(ohf.astype(BF16), tri_ref[...], preferred_element_type=F32)
        slot = jnp.sum(jnp.where(oh, before + base, 0.0), axis=0, keepdims=True)
        dest_ref[k:k + 1, :] = slot.astype(jnp.int32)
        base = base + jnp.sum(ohf, axis=1, keepdims=True)
    carry_ref[...] = base - start_ref[...]


def _rank(e_t, seg_start):
    _, N = e_t.shape
    TM = ROW_TILE
    tri = (jnp.arange(TM)[:, None] < jnp.arange(TM)[None, :]).astype(BF16)
    return pl.pallas_call(
        _rank_kernel,
        out_shape=jax.ShapeDtypeStruct((2, N), jnp.int32),
        grid=(N // TM,),
        in_specs=[
            pl.BlockSpec((2, TM), lambda i: (0, i)),
            pl.BlockSpec((N_EXPERTS, 1), lambda i: (0, 0)),
            pl.BlockSpec((TM, TM), lambda i: (0, 0)),
        ],
        out_specs=pl.BlockSpec((2, TM), lambda i: (0, i)),
        scratch_shapes=[pltpu.VMEM((N_EXPERTS, 1), F32)],
        compiler_params=_cparams(("arbitrary",)),
        name="moe_rank",
    )(e_t, seg_start.astype(F32).reshape(N_EXPERTS, 1), tri)


def _store_token_tiles(dst_ref, val):
    rows, d = val.shape
    nch = d // LANES
    for c in range(nch):
        dst_ref[pl.ds(c, rows, stride=nch), :] = val[:, c * LANES:(c + 1) * LANES]


def _load_token_tiles(src_ref, rows):
    nch = src_ref.shape[0] // rows
    return jnp.concatenate([src_ref[pl.ds(c, rows, stride=nch), :] for c in range(nch)], axis=1)


def _token_copy(src_ref, src_tok, dst_ref, dst_tok, sem, nch):
    return pltpu.make_async_copy(src_ref.at[pl.ds(pl.multiple_of(src_tok * nch, nch), nch)],
                                 dst_ref.at[pl.ds(pl.multiple_of(dst_tok * nch, nch), nch)], sem)


def _dispatch_kernel(dest_ref, h_ref, xs_in_ref, xs_ref, hs_ref, sem):
    del xs_in_ref
    TM, D = h_ref.shape
    nch = D // LANES
    _store_token_tiles(hs_ref, h_ref[...])

    def issue(t, carry):
        _token_copy(hs_ref, t, xs_ref, dest_ref[0, 0, t], sem, nch).start()
        _token_copy(hs_ref, t, xs_ref, dest_ref[0, 1, t], sem, nch).start()
        return carry

    lax.fori_loop(0, TM, issue, 0, unroll=8)
    for _ in range(2):
        pltpu.make_async_copy(hs_ref, xs_ref.at[pl.ds(0, TM * nch)], sem).wait()


def _dispatch(h, dest_t, n_slots):
    N, D = h.shape
    TM = ROW_TILE
    nch = D // LANES
    dest3 = dest_t.reshape(2, N // TM, TM).transpose(1, 0, 2)
    xs0 = jnp.zeros((n_slots * nch, LANES), h.dtype)
    return pl.pallas_call(
        _dispatch_kernel,
        out_shape=jax.ShapeDtypeStruct((n_slots * nch, LANES), h.dtype),
        grid=(N // TM,),
        in_specs=[
            pl.BlockSpec((1, 2, TM), lambda i: (i, 0, 0), memory_space=pltpu.SMEM),
            pl.BlockSpec((TM, D), lambda i: (i, 0)),
            pl.BlockSpec(memory_space=pl.ANY),
        ],
        out_specs=pl.BlockSpec(memory_space=pl.ANY),
        scratch_shapes=[pltpu.VMEM((TM * nch, LANES), h.dtype), pltpu.SemaphoreType.DMA],
        input_output_aliases={2: 0},
        compiler_params=_cparams(("arbitrary",)),
        name="moe_dispatch",
    )(dest3, h, xs0)


def _expert_kernel(be_ref, nu_ref, xs_ref, w1_ref, w3_ref, w2_ref, ys_ref, w1b, w3b, w2b):
    i = pl.program_id(0)
    BLK = EXPERT_BLOCK

    @pl.when(i < nu_ref[0])
    def _():
        prev = be_ref[jnp.maximum(i - 1, 0)]

        @pl.when((i == 0) | (be_ref[i] != prev))
        def _():
            w1b[...] = w1_ref[0].astype(BF16)
            w3b[...] = w3_ref[0].astype(BF16)
            w2b[...] = w2_ref[0].astype(BF16)

        xb = _load_token_tiles(xs_ref, BLK).astype(BF16)
        a = jnp.dot(xb, w1b[...], preferred_element_type=F32)
        b = jnp.dot(xb, w3b[...], preferred_element_type=F32)
        act = a * (1.0 / (1.0 + jnp.exp(-a))) * b
        _store_token_tiles(ys_ref, jnp.dot(act.astype(BF16), w2b[...], preferred_element_type=F32))

    @pl.when(i >= nu_ref[0])
    def _():
        ys_ref[...] = jnp.zeros(ys_ref.shape, F32)


def _experts(xs, block_expert, n_used, w1, w3, w2, layer):
    D, DE = w1.shape[2], w1.shape[3]
    nch = D // LANES
    BLK = EXPERT_BLOCK
    n_blocks = xs.shape[0] // (BLK * nch)

    def xmap(i, be, nu):
        return (jnp.minimum(i, jnp.maximum(nu[0] - 1, 0)), 0)

    def wmap(i, be, nu):
        return (layer, be[i], 0, 0)

    return pl.pallas_call(
        _expert_kernel,
        out_shape=jax.ShapeDtypeStruct(xs.shape, F32),
        grid_spec=pltpu.PrefetchScalarGridSpec(
            num_scalar_prefetch=2,
            grid=(n_blocks,),
            in_specs=[
                pl.BlockSpec((BLK * nch, LANES), xmap),
                pl.BlockSpec((None, 1, D, DE), wmap),
                pl.BlockSpec((None, 1, D, DE), wmap),
                pl.BlockSpec((None, 1, DE, D), wmap),
            ],
            out_specs=pl.BlockSpec((BLK * nch, LANES), lambda i, be, nu: (i, 0)),
            scratch_shapes=[
                pltpu.VMEM((D, DE), BF16),
                pltpu.VMEM((D, DE), BF16),
                pltpu.VMEM((DE, D), BF16),
            ],
        ),
        compiler_params=_cparams(("arbitrary",)),
        name="moe_experts",
    )(block_expert, n_used, xs, w1, w3, w2)


def _combine_kernel(dest_ref, x_ref, gate_ref, ys_ref, g_ref, o_ref, buf0, buf1, sem, *, final_norm):
    TM, D = x_ref.shape
    nch = D // LANES

    def issue(t, carry):
        _token_copy(ys_ref, dest_ref[0, 0, t], buf0, t, sem, nch).start()
        _token_copy(ys_ref, dest_ref[0, 1, t], buf1, t, sem, nch).start()
        return carry

    lax.fori_loop(0, TM, issue, 0, unroll=8)
    pltpu.make_async_copy(ys_ref.at[pl.ds(0, TM * nch)], buf0, sem).wait()
    pltpu.make_async_copy(ys_ref.at[pl.ds(0, TM * nch)], buf1, sem).wait()
    gate = gate_ref[...]
    y = x_ref[...] + gate[:, 0:1] * _load_token_tiles(buf0, TM) + gate[:, 1:2] * _load_token_tiles(buf1, TM)
    if final_norm:
        y = _rms(y, g_ref[...], RMS_EPS)
    o_ref[...] = y


def _combine(x2d, dest_t, gate_t, ys, g, final_norm):
    N, D = x2d.shape
    TM = ROW_TILE
    nch = D // LANES
    dest3 = dest_t.reshape(2, N // TM, TM).transpose(1, 0, 2)
    gate = gate_t.T
    return pl.pallas_call(
        functools.partial(_combine_kernel, final_norm=final_norm),
        out_shape=jax.ShapeDtypeStruct((N, D), F32),
        grid=(N // TM,),
        in_specs=[
            pl.BlockSpec((1, 2, TM), lambda i: (i, 0, 0), memory_space=pltpu.SMEM),
            pl.BlockSpec((TM, D), lambda i: (i, 0)),
            pl.BlockSpec((TM, 2), lambda i: (i, 0)),
            pl.BlockSpec(memory_space=pl.ANY),
            pl.BlockSpec((1, D), lambda i: (0, 0)),
        ],
        out_specs=pl.BlockSpec((TM, D), lambda i: (i, 0)),
        scratch_shapes=[pltpu.VMEM((TM * nch, LANES), F32), pltpu.VMEM((TM * nch, LANES), F32),
                        pltpu.SemaphoreType.DMA],
        compiler_params=_cparams(("arbitrary",)),
        name="moe_combine",
    )(dest3, x2d, gate, ys, g.reshape(1, D))


def _moe_layer(x2d, norm_g, wg, bg, we, be, w1, w3, w2, layer, out_g, final_norm):
    N, D = x2d.shape
    BLK = EXPERT_BLOCK
    n_slots = 2 * N + N_EXPERTS * BLK
    h, e_t, gate_t, cnt = _router(x2d, norm_g, wg, bg, we, be)
    counts = cnt[:, 0].astype(jnp.int32)
    padded = ((counts + BLK - 1) // BLK) * BLK
    seg_end = jnp.cumsum(padded)
    seg_start = seg_end - padded
    n_blocks = n_slots // BLK
    block_row0 = jnp.arange(n_blocks, dtype=jnp.int32) * BLK
    block_expert = jnp.minimum(
        jnp.sum((seg_end[None, :] <= block_row0[:, None]).astype(jnp.int32), axis=1), N_EXPERTS - 1)
    n_used = (seg_end[-1:] // BLK).astype(jnp.int32)
    dest_t = _rank(e_t, seg_start)
    xs = _dispatch(h, dest_t, n_slots)
    ys = _experts(xs, block_expert, n_used, w1, w3, w2, layer)
    return _combine(x2d, dest_t, gate_t, ys, out_g, final_norm)


def _qkv_kernel(x_ref, g_ref, wq_ref, wk_ref, wv_ref, q_ref, k_ref, v_ref):
    h = _rms(x_ref[...], g_ref[...], RMS_EPS).astype(BF16)
    q = jnp.dot(h, wq_ref[...], preferred_element_type=F32)
    q_ref[...] = (q * (LOG2E * HEAD_DIM ** -0.5)).astype(BF16)
    k_ref[...] = jnp.dot(h, wk_ref[...], preferred_element_type=F32).astype(BF16)
    v_ref[...] = jnp.dot(h, wv_ref[...], preferred_element_type=F32).astype(BF16)


def _qkv(x2d, g, wq, wk, wv):
    N, D = x2d.shape
    DA = wq.shape[1]
    TM = ROW_TILE
    wspec = pl.BlockSpec((D, DA), lambda i: (0, 0))
    ospec = pl.BlockSpec((TM, DA), lambda i: (i, 0))
    osh = jax.ShapeDtypeStruct((N, DA), BF16)
    return pl.pallas_call(
        _qkv_kernel,
        out_shape=(osh, osh, osh),
        grid=(N // TM,),
        in_specs=[pl.BlockSpec((TM, D), lambda i: (i, 0)), pl.BlockSpec((1, D), lambda i: (0, 0)),
                  wspec, wspec, wspec],
        out_specs=(ospec, ospec, ospec),
        compiler_params=_cparams(("arbitrary",)),
        name="attn_qkv",
    )(x2d, g.reshape(1, D), wq.astype(BF16), wk.astype(BF16), wv.astype(BF16))


def _attn_kernel(slope_ref, q_ref, k_ref, v_ref, db_ref, lam_ref, sg_ref, o_ref,
                 qz_ref, s_ref, p_ref, m_ref, l_ref, acc_ref, *, lam_init):
    TQ = q_ref.shape[0]
    TK = ATTN_TK
    hd = pl.program_id(1)
    qi = pl.program_id(2)
    slope = slope_ref[hd]

    q = q_ref[...]
    lane = lax.broadcasted_iota(jnp.int32, q.shape, 1)
    zero = jnp.zeros_like(q)
    qz_ref[0:TQ, :] = jnp.where(lane < HEAD_DIM, q, zero)
    qz_ref[TQ:2 * TQ, :] = jnp.where(lane >= HEAD_DIM, q, zero)
    m_ref[...] = jnp.full(m_ref.shape, -jnp.inf, F32)
    l_ref[...] = jnp.zeros(l_ref.shape, F32)
    acc_ref[...] = jnp.zeros(acc_ref.shape, F32)
    LW = acc_ref.shape[1]
    NC = TK // LW
    kpos = lax.broadcasted_iota(jnp.int32, (1, TK), 1)
    q0 = qi * TQ

    def tile(j, diagonal):
        koff = pl.multiple_of(j * TK, TK)
        s = lax.dot_general(qz_ref[...], k_ref[pl.ds(koff, TK), :], (((1,), (1,)), ((), ())),
                            preferred_element_type=F32)
        if diagonal:
            s = jnp.concatenate([s[0:TQ] + db_ref[...], s[TQ:2 * TQ] + db_ref[...]], axis=0)
        else:
            s = s + slope * (kpos + (j * TK - q0)).astype(F32)
        s_ref[...] = s
        mx = s[:, 0:LW]
        for c in range(1, NC):
            mx = jnp.maximum(mx, s[:, c * LW:(c + 1) * LW])
        m_old = m_ref[...]
        m_new = jnp.maximum(m_old, jnp.max(mx, axis=-1, keepdims=True))
        alpha = jnp.exp2(m_old - m_new)
        m_ref[...] = m_new
        psum = None
        for c in range(NC):
            p = jnp.exp2(s_ref[:, c * LW:(c + 1) * LW] - m_new)
            p_ref[:, c * LW:(c + 1) * LW] = p.astype(BF16)
            psum = p if psum is None else psum + p
        l_ref[...] = alpha * l_ref[...] + psum
        acc_ref[...] = alpha * acc_ref[...] + jnp.dot(p_ref[...], v_ref[pl.ds(koff, TK), :],
                                                      preferred_element_type=F32)

    def past_tile(j, carry):
        tile(j, False)
        return carry

    lax.fori_loop(0, qi, past_tile, 0)
    tile(qi, True)

    lamv = lam_ref[...]
    lam = (jnp.exp(jnp.sum(lamv[0:1] * lamv[1:2], axis=-1, keepdims=True))
           - jnp.exp(jnp.sum(lamv[2:3] * lamv[3:4], axis=-1, keepdims=True)) + lam_init)
    on = acc_ref[...] / jnp.sum(l_ref[...], axis=-1, keepdims=True)
    o = on[0:TQ] - lam * on[TQ:2 * TQ]
    o = _rms(o, sg_ref[...], SUBLN_EPS) * (1.0 - lam_init)
    o_ref[...] = o.astype(o_ref.dtype)


def _attention(q, k, v, lamv, subln_g, lam_init):
    B, S, DA = q.shape
    HW = 2 * HEAD_DIM
    TQ = ATTN_TQ
    TK = ATTN_TK
    assert TK == TQ and S % TK == 0 and TK % CHUNK == 0
    slopes = jnp.array([LOG2E * 2.0 ** (-8.0 * (h + 1) / N_HEADS) for h in range(N_HEADS)], F32)
    qr = jnp.arange(TQ, dtype=jnp.int32)[:, None]
    kc = jnp.arange(TK, dtype=jnp.int32)[None, :]
    rel = jnp.where(kc <= qr, kc, 2 * qr - kc).astype(F32)
    diag_bias = jnp.where((kc // CHUNK <= qr // CHUNK)[None], slopes[:, None, None] * rel[None], NEG_INF)
    return pl.pallas_call(
        functools.partial(_attn_kernel, lam_init=lam_init),
        out_shape=jax.ShapeDtypeStruct((B, S, DA), BF16),
        grid_spec=pltpu.PrefetchScalarGridSpec(
            num_scalar_prefetch=1,
            grid=(B, N_HEADS, S // TQ),
            in_specs=[
                pl.BlockSpec((None, TQ, HW), lambda b, h, i, sl: (b, i, h)),
                pl.BlockSpec((None, S, HW), lambda b, h, i, sl: (b, 0, h)),
                pl.BlockSpec((None, S, HW), lambda b, h, i, sl: (b, 0, h)),
                pl.BlockSpec((None, TQ, TK), lambda b, h, i, sl: (h, 0, 0)),
                pl.BlockSpec((4, HEAD_DIM), lambda b, h, i, sl: (0, 0)),
                pl.BlockSpec((1, HW), lambda b, h, i, sl: (0, 0)),
            ],
            out_specs=pl.BlockSpec((None, TQ, HW), lambda b, h, i, sl: (b, i, h)),
            scratch_shapes=[
                pltpu.VMEM((2 * TQ, HW), BF16),
                pltpu.VMEM((2 * TQ, TK), F32),
                pltpu.VMEM((2 * TQ, TK), BF16),
                pltpu.VMEM((2 * TQ, HW), F32),
                pltpu.VMEM((2 * TQ, HW), F32),
                pltpu.VMEM((2 * TQ, HW), F32),
            ],
        ),
        compiler_params=_cparams(("arbitrary", "arbitrary", "arbitrary")),
        name="diff_attention",
    )(slopes, q, k, v, diag_bias, lamv, subln_g.reshape(1, HW))


def _wo_kernel(o_ref, w_ref, x_ref, out_ref):
    out_ref[...] = x_ref[...] + jnp.dot(o_ref[...], w_ref[...], preferred_element_type=F32)


def _wo(o2d, wo, x2d):
    N, D = x2d.shape
    DA = o2d.shape[1]
    TM = ROW_TILE
    return pl.pallas_call(
        _wo_kernel,
        out_shape=jax.ShapeDtypeStruct((N, D), F32),
        grid=(N // TM,),
        in_specs=[pl.BlockSpec((TM, DA), lambda i: (i, 0)), pl.BlockSpec((DA, D), lambda i: (0, 0)),
                  pl.BlockSpec((TM, D), lambda i: (i, 0))],
        out_specs=pl.BlockSpec((TM, D), lambda i: (i, 0)),
        compiler_params=_cparams(("arbitrary",)),
        name="attn_wo",
    )(o2d, wo.astype(BF16), x2d)


def _attn_layer(x, g, wq, wk, wv, wo, lq1, lk1, lq2, lk2, subln_g, layer_idx):
    B, S, D = x.shape
    x2d = x.reshape(B * S, D)
    q, k, v = _qkv(x2d, g, wq, wk, wv)
    DA = q.shape[1]
    lam_init = 0.8 - 0.6 * math.exp(-0.3 * layer_idx)
    lamv = jnp.stack([lq1, lk1, lq2, lk2]).astype(F32)
    o = _attention(q.reshape(B, S, DA), k.reshape(B, S, DA), v.reshape(B, S, DA), lamv, subln_g, lam_init)
    return _wo(o.reshape(B * S, DA), wo, x2d).reshape(B, S, D)


def kernel(x, norm_mix_g, norm_ffn_g, pool_w, pool_scale, attn_wq, attn_wk, attn_wv, attn_wo,
           attn_lq1, attn_lk1, attn_lq2, attn_lk2, attn_subln_g, moe_wg, moe_bg, moe_we, moe_be,
           moe_w1, moe_w3, moe_w2, final_g):
    B, S, D = x.shape
    depth = norm_mix_g.shape[0]
    for i in range(depth):
        j = i // 2
        if i % 2 == 0:
            x = _pool_layer(x, norm_mix_g[i], pool_w[j], pool_scale[j])
        else:
            x = _attn_layer(x, norm_mix_g[i], attn_wq[j], attn_wk[j], attn_wv[j], attn_wo[j],
                            attn_lq1[j], attn_lk1[j], attn_lq2[j], attn_lk2[j], attn_subln_g[j], i)
        last = i == depth - 1
        x = _moe_layer(x.reshape(B * S, D), norm_ffn_g[i], moe_wg[i], moe_bg[i], moe_we[i], moe_be[i],
                       moe_w1, moe_w3, moe_w2, i, final_g, last).reshape(B, S, D)
    return x
```

```python
import functools
import math

import jax
import jax.numpy as jnp
from jax import lax
from jax.experimental import pallas as pl
from jax.experimental.pallas import tpu as pltpu

F32 = jnp.float32
BF16 = jnp.bfloat16

RMS_EPS = 1e-6
SUBLN_EPS = 1e-5
POOL_WINDOWS = (2, 4, 8, 16)
CHUNK = 64
N_HEADS = 8
HEAD_DIM = 64
N_GROUPS = 4
EXPERTS_PER_GROUP = 8
N_EXPERTS = N_GROUPS * EXPERTS_PER_GROUP
NEG_INF = -1e30
LOG2E = math.log2(math.e)

VMEM_LIMIT_BYTES = 48 * 1024 * 1024
POOL_TILE = 256
ROW_TILE = 512
EXPERT_BLOCK = 512
ATTN_TQ = 512
ATTN_TK = 512
LANES = 128
LOGIT_ROWS = 40


def _cparams(sem):
    return pltpu.CompilerParams(dimension_semantics=sem, vmem_limit_bytes=VMEM_LIMIT_BYTES)


def _rms(x, g, eps):
    return x * lax.rsqrt(jnp.mean(x * x, axis=-1, keepdims=True) + eps) * g


def _pool_kernel(x_ref, g_ref, band_ref, icnt_ref, w_ref, sc_ref, o_ref, hh_ref):
    T = POOL_TILE
    si = pl.program_id(1)

    @pl.when(si == 0)
    def _():
        hh_ref[0:T, :] = jnp.zeros((T, hh_ref.shape[1]), BF16)

    @pl.when(si > 0)
    def _():
        hh_ref[0:T, :] = hh_ref[T:2 * T, :]

    x = x_ref[...]
    h = _rms(x, g_ref[...], RMS_EPS)
    hh_ref[T:2 * T, :] = h.astype(BF16)
    gw = w_ref.shape[1]
    outs = []
    for gi in range(len(POOL_WINDOWS)):
        cols = slice(gi * gw, (gi + 1) * gw)
        wsum = jnp.dot(band_ref[gi], hh_ref[:, cols], preferred_element_type=F32)
        mix = wsum * icnt_ref[gi] - h[:, cols]
        outs.append(jnp.dot(mix.astype(BF16), w_ref[gi], preferred_element_type=F32))
    o_ref[...] = x + jnp.concatenate(outs, axis=-1) * sc_ref[...]


def _pool_layer(x, g, w, scale):
    B, S, D = x.shape
    T = POOL_TILE
    G = len(POOL_WINDOWS)
    r = jnp.arange(T)[:, None]
    c = jnp.arange(2 * T)[None, :]
    band = jnp.stack([((c <= r + T) & (c > r + T - win)) for win in POOL_WINDOWS]).astype(BF16)
    pos = jnp.arange(S, dtype=jnp.int32)
    icnt = jnp.stack([1.0 / jnp.minimum(pos + 1, win).astype(F32) for win in POOL_WINDOWS])
    icnt = icnt.reshape(G, S, 1)
    return pl.pallas_call(
        _pool_kernel,
        out_shape=jax.ShapeDtypeStruct((B, S, D), F32),
        grid=(B, S // T),
        in_specs=[
            pl.BlockSpec((None, T, D), lambda b, s: (b, s, 0)),
            pl.BlockSpec((1, D), lambda b, s: (0, 0)),
            pl.BlockSpec((G, T, 2 * T), lambda b, s: (0, 0, 0)),
            pl.BlockSpec((G, T, 1), lambda b, s: (0, s, 0)),
            pl.BlockSpec((G, D // G, D // G), lambda b, s: (0, 0, 0)),
            pl.BlockSpec((1, D), lambda b, s: (0, 0)),
        ],
        out_specs=pl.BlockSpec((None, T, D), lambda b, s: (b, s, 0)),
        scratch_shapes=[pltpu.VMEM((2 * T, D), BF16)],
        compiler_params=_cparams(("arbitrary", "arbitrary")),
        name="pool_mixer",
    )(x, g.reshape(1, D), band, icnt, w.astype(BF16), scale.reshape(1, D))


def _router_kernel(x_ref, g_ref, wt_ref, b_ref, h_ref, e_ref, gate_ref, cnt_ref):
    i = pl.program_id(0)
    x = x_ref[...]
    h = _rms(x, g_ref[...], RMS_EPS)
    _store_token_tiles(h_ref, h)
    lt = lax.dot_general(wt_ref[...], h, (((1,), (1,)), ((), ())),
                         precision=lax.Precision.HIGHEST, preferred_element_type=F32)
    lt = lt + b_ref[...]
    g0, g1, g2, g3 = lt[0:1], lt[1:2], lt[2:3], lt[3:4]
    gmax = jnp.maximum(jnp.maximum(g0, g1), jnp.maximum(g2, g3))
    gidx = jnp.where(g0 == gmax, 0, jnp.where(g1 == gmax, 1, jnp.where(g2 == gmax, 2, 3)))
    gsum = jnp.exp(g0 - gmax) + jnp.exp(g1 - gmax) + jnp.exp(g2 - gmax) + jnp.exp(g3 - gmax)
    g_p = 1.0 / gsum
    E = EXPERTS_PER_GROUP
    e_in = jnp.where(gidx == 0, lt[8:8 + E],
                     jnp.where(gidx == 1, lt[8 + E:8 + 2 * E],
                               jnp.where(gidx == 2, lt[8 + 2 * E:8 + 3 * E], lt[8 + 3 * E:8 + 4 * E])))
    io = lax.broadcasted_iota(jnp.int32, e_in.shape, 0)
    v0 = jnp.max(e_in, axis=0, keepdims=True)
    i0 = jnp.min(jnp.where(e_in == v0, io, E), axis=0, keepdims=True)
    rest = jnp.where(io == i0, -jnp.inf, e_in)
    v1 = jnp.max(rest, axis=0, keepdims=True)
    i1 = jnp.min(jnp.where(rest == v1, io, E), axis=0, keepdims=True)
    ex = jnp.exp(v1 - v0)
    p0 = 1.0 / (1.0 + ex)
    e0 = gidx * E + i0
    e1 = gidx * E + i1
    e_ref[0:1, :] = e0
    e_ref[1:2, :] = e1
    gate_ref[0:1, :] = g_p * p0
    gate_ref[1:2, :] = g_p * (ex * p0)
    io32 = lax.broadcasted_iota(jnp.int32, (N_EXPERTS, x.shape[0]), 0)
    oh = (io32 == e0).astype(F32) + (io32 == e1).astype(F32)
    tile_cnt = jnp.sum(oh, axis=1, keepdims=True)

    @pl.when(i == 0)
    def _():
        cnt_ref[...] = jnp.zeros(cnt_ref.shape, F32)

    cnt_ref[...] += jnp.broadcast_to(tile_cnt, cnt_ref.shape)


def _router(x2d, g, wg, bg, we, be):
    N, D = x2d.shape
    TM = ROW_TILE
    nch = D // LANES
    wt = jnp.zeros((LOGIT_ROWS, D), F32).at[0:N_GROUPS].set(wg.T).at[8:8 + N_EXPERTS].set(we.T)
    bt = jnp.zeros((LOGIT_ROWS, 1), F32).at[0:N_GROUPS, 0].set(bg).at[8:8 + N_EXPERTS, 0].set(be)
    return pl.pallas_call(
        _router_kernel,
        out_shape=(
            jax.ShapeDtypeStruct((N * nch, LANES), F32),
            jax.ShapeDtypeStruct((2, N), jnp.int32),
            jax.ShapeDtypeStruct((2, N), F32),
            jax.ShapeDtypeStruct((N_EXPERTS, 128), F32),
        ),
        grid=(N // TM,),
        in_specs=[
            pl.BlockSpec((TM, D), lambda i: (i, 0)),
            pl.BlockSpec((1, D), lambda i: (0, 0)),
            pl.BlockSpec((LOGIT_ROWS, D), lambda i: (0, 0)),
            pl.BlockSpec((LOGIT_ROWS, 1), lambda i: (0, 0)),
        ],
        out_specs=(
            pl.BlockSpec((TM * nch, LANES), lambda i: (i, 0)),
            pl.BlockSpec((2, TM), lambda i: (0, i)),
            pl.BlockSpec((2, TM), lambda i: (0, i)),
            pl.BlockSpec((N_EXPERTS, 128), lambda i: (0, 0)),
        ),
        compiler_params=_cparams(("arbitrary",)),
        name="moe_router",
    )(x2d, g.reshape(1, D), wt, bt)


def _rank_kernel(e_ref, start_ref, tri_ref, dest_ref, carry_ref):
    i = pl.program_id(0)

    @pl.when(i == 0)
    def _():
        carry_ref[...] = jnp.zeros(carry_ref.shape, F32)

    TM = e_ref.shape[1]
    io32 = lax.broadcasted_iota(jnp.int32, (N_EXPERTS, TM), 0)
    base = carry_ref[...] + start_ref[...]
    for k in range(2):
        oh = io32 == e_ref[k:k + 1, :]
        ohf = oh.astype(F32)
        before = jnp.dot(ohf.astype(BF16), tri_ref[...], preferred_element_type=F32)
        slot = jnp.sum(jnp.where(oh, before + base, 0.0), axis=0, keepdims=True)
        dest_ref[k:k + 1, :] = slot.astype(jnp.int32)
        base = base + jnp.sum(ohf, axis=1, keepdims=True)
    carry_ref[...] = base - start_ref[...]


def _rank(e_t, seg_start):
    _, N = e_t.shape
    TM = ROW_TILE
    tri = (jnp.arange(TM)[:, None] < jnp.arange(TM)[None, :]).astype(BF16)
    return pl.pallas_call(
        _rank_kernel,
        out_shape=jax.ShapeDtypeStruct((2, N), jnp.int32),
        grid=(N // TM,),
        in_specs=[
            pl.BlockSpec((2, TM), lambda i: (0, i)),
            pl.BlockSpec((N_EXPERTS, 1), lambda i: (0, 0)),
            pl.BlockSpec((TM, TM), lambda i: (0, 0)),
        ],
        out_specs=pl.BlockSpec((2, TM), lambda i: (0, i)),
        scratch_shapes=[pltpu.VMEM((N_EXPERTS, 1), F32)],
        compiler_params=_cparams(("arbitrary",)),
        name="moe_rank",
    )(e_t, seg_start.astype(F32).reshape(N_EXPERTS, 1), tri)


def _store_token_tiles(dst_ref, val):
    rows, d = val.shape
    nch = d // LANES
    for c in range(nch):
        dst_ref[pl.ds(c, rows, stride=nch), :] = val[:, c * LANES:(c + 1) * LANES]


def _load_token_tiles(src_ref, rows):
    nch = src_ref.shape[0] // rows
    return jnp.concatenate([src_ref[pl.ds(c, rows, stride=nch), :] for c in range(nch)], axis=1)


def _token_copy(src_ref, src_tok, dst_ref, dst_tok, sem, nch):
    return pltpu.make_async_copy(src_ref.at[pl.ds(pl.multiple_of(src_tok * nch, nch), nch)],
                                 dst_ref.at[pl.ds(pl.multiple_of(dst_tok * nch, nch), nch)], sem)


def _expert_kernel(be_ref, nu_ref, src0_ref, srcn_ref, dst_ref, h_ref, w1_ref, w3_ref, w2_ref, y_ref,
                   xbuf, ybuf, w1b, w3b, w2b, gsem, ssem):
    i = pl.program_id(0)
    BLK = EXPERT_BLOCK
    nch = xbuf.shape[1] // BLK
    nu = nu_ref[0]
    slot = i % 2

    def gather(idx_ref, to_slot):
        def issue(j, carry):
            _token_copy(h_ref, idx_ref[0, 0, j], xbuf.at[to_slot], j, gsem.at[to_slot], nch).start()
            return carry
        lax.fori_loop(0, BLK, issue, 0, unroll=8)

    def wait_scatter(of_slot):
        pltpu.make_async_copy(ybuf.at[of_slot], y_ref.at[pl.ds(0, BLK * nch)], ssem.at[of_slot]).wait()

    @pl.when(i < nu)
    def _():
        @pl.when(i == 0)
        def _():
            gather(src0_ref, 0)

        @pl.when(i + 1 < nu)
        def _():
            gather(srcn_ref, 1 - slot)

        prev = be_ref[jnp.maximum(i - 1, 0)]

        @pl.when((i == 0) | (be_ref[i] != prev))
        def _():
            w1b[...] = w1_ref[0].astype(BF16)
            w3b[...] = w3_ref[0].astype(BF16)
            w2b[...] = w2_ref[0].astype(BF16)

        pltpu.make_async_copy(h_ref.at[pl.ds(0, BLK * nch)], xbuf.at[slot], gsem.at[slot]).wait()
        xb = _load_token_tiles(xbuf.at[slot], BLK).astype(BF16)
        a = jnp.dot(xb, w1b[...], preferred_element_type=F32)
        b = jnp.dot(xb, w3b[...], preferred_element_type=F32)
        act = a * (1.0 / (1.0 + jnp.exp(-a))) * b
        y = jnp.dot(act.astype(BF16), w2b[...], preferred_element_type=F32)

        @pl.when(i >= 2)
        def _():
            wait_scatter(slot)

        _store_token_tiles(ybuf.at[slot], y)

        def issue(j, carry):
            _token_copy(ybuf.at[slot], j, y_ref, dst_ref[0, 0, j], ssem.at[slot], nch).start(priority=1)
            return carry
        lax.fori_loop(0, BLK, issue, 0, unroll=8)

        @pl.when(i == nu - 1)
        def _():
            @pl.when(i >= 1)
            def _():
                wait_scatter(1 - slot)
            wait_scatter(slot)
            ybuf[0] = jnp.zeros(ybuf.shape[1:], F32)
            for spare_block in (1, 2):
                first = y_ref.shape[0] - spare_block * BLK * nch
                spare = pltpu.make_async_copy(ybuf.at[0], y_ref.at[pl.ds(first, BLK * nch)], ssem.at[0])
                spare.start()
                spare.wait()


def _experts(h, src_idx, dst_idx, block_expert, n_used, w1, w3, w2, layer, n_tokens):
    D, DE = w1.shape[2], w1.shape[3]
    nch = D // LANES
    BLK = EXPERT_BLOCK
    n_blocks = src_idx.shape[0]
    n_rows = 2 * n_tokens + 2 * BLK

    def wmap(i, be, nu):
        return (layer, be[i], 0, 0)

    idx_block = (1, 1, BLK)
    return pl.pallas_call(
        _expert_kernel,
        out_shape=jax.ShapeDtypeStruct((n_rows * nch, LANES), F32),
        grid_spec=pltpu.PrefetchScalarGridSpec(
            num_scalar_prefetch=2,
            grid=(n_blocks,),
            in_specs=[
                pl.BlockSpec(idx_block, lambda i, be, nu: (0, 0, 0), memory_space=pltpu.SMEM),
                pl.BlockSpec(idx_block, lambda i, be, nu: (jnp.minimum(i + 1, n_blocks - 1), 0, 0),
                             memory_space=pltpu.SMEM),
                pl.BlockSpec(idx_block, lambda i, be, nu: (i, 0, 0), memory_space=pltpu.SMEM),
                pl.BlockSpec(memory_space=pl.ANY),
                pl.BlockSpec((None, 1, D, DE), wmap),
                pl.BlockSpec((None, 1, D, DE), wmap),
                pl.BlockSpec((None, 1, DE, D), wmap),
            ],
            out_specs=pl.BlockSpec(memory_space=pl.ANY),
            scratch_shapes=[
                pltpu.VMEM((2, BLK * nch, LANES), F32),
                pltpu.VMEM((2, BLK * nch, LANES), F32),
                pltpu.VMEM((D, DE), BF16),
                pltpu.VMEM((D, DE), BF16),
                pltpu.VMEM((DE, D), BF16),
                pltpu.SemaphoreType.DMA((2,)),
                pltpu.SemaphoreType.DMA((2,)),
            ],
        ),
        compiler_params=_cparams(("arbitrary",)),
        name="moe_experts",
    )(block_expert, n_used, src_idx, src_idx, dst_idx, h, w1, w3, w2)


def _combine_kernel(x_ref, gate_ref, y0_ref, y1_ref, g_ref, o_ref, *, final_norm):
    TM = x_ref.shape[0]
    gate = gate_ref[...]
    y = x_ref[...] + gate[:, 0:1] * _load_token_tiles(y0_ref, TM) + gate[:, 1:2] * _load_token_tiles(y1_ref, TM)
    if final_norm:
        y = _rms(y, g_ref[...], RMS_EPS)
    o_ref[...] = y


def _combine(x2d, gate_t, y, g, final_norm):
    N, D = x2d.shape
    TM = ROW_TILE
    nch = D // LANES
    n_tiles = N // TM
    gate = gate_t.T
    return pl.pallas_call(
        functools.partial(_combine_kernel, final_norm=final_norm),
        out_shape=jax.ShapeDtypeStruct((N, D), F32),
        grid=(n_tiles,),
        in_specs=[
            pl.BlockSpec((TM, D), lambda i: (i, 0)),
            pl.BlockSpec((TM, 2), lambda i: (i, 0)),
            pl.BlockSpec((TM * nch, LANES), lambda i: (i, 0)),
            pl.BlockSpec((TM * nch, LANES), lambda i: (n_tiles + i, 0)),
            pl.BlockSpec((1, D), lambda i: (0, 0)),
        ],
        out_specs=pl.BlockSpec((TM, D), lambda i: (i, 0)),
        compiler_params=_cparams(("arbitrary",)),
        name="moe_combine",
    )(x2d, gate, y, y, g.reshape(1, D))


def _moe_layer(x2d, norm_g, wg, bg, we, be, w1, w3, w2, layer, out_g, final_norm):
    N, D = x2d.shape
    BLK = EXPERT_BLOCK
    n_slots = 2 * N + N_EXPERTS * BLK
    h, e_t, gate_t, cnt = _router(x2d, norm_g, wg, bg, we, be)
    counts = cnt[:, 0].astype(jnp.int32)
    padded = ((counts + BLK - 1) // BLK) * BLK
    seg_end = jnp.cumsum(padded)
    seg_start = seg_end - padded
    n_blocks = n_slots // BLK
    block_row0 = jnp.arange(n_blocks, dtype=jnp.int32) * BLK
    block_expert = jnp.minimum(
        jnp.sum((seg_end[None, :] <= block_row0[:, None]).astype(jnp.int32), axis=1), N_EXPERTS - 1)
    n_used = (seg_end[-1:] // BLK).astype(jnp.int32)
    dest_t = _rank(e_t, seg_start)
    pair = jnp.full((n_slots,), -1, jnp.int32).at[dest_t.reshape(-1)].set(jnp.arange(2 * N, dtype=jnp.int32))
    slot_id = jnp.arange(n_slots, dtype=jnp.int32)
    spare_row = 2 * N + ((slot_id // BLK) % 2) * BLK + slot_id % BLK
    src_idx = jnp.where(pair < 0, 0, pair % N).reshape(n_blocks, 1, BLK)
    dst_idx = jnp.where(pair < 0, spare_row, pair).reshape(n_blocks, 1, BLK)
    y = _experts(h, src_idx, dst_idx, block_expert, n_used, w1, w3, w2, layer, N)
    return _combine(x2d, gate_t, y, out_g, final_norm)


def _qkv_kernel(x_ref, g_ref, wq_ref, wk_ref, wv_ref, q_ref, k_ref, v_ref):
    h = _rms(x_ref[...], g_ref[...], RMS_EPS).astype(BF16)
    q = jnp.dot(h, wq_ref[...], preferred_element_type=F32)
    q_ref[...] = (q * (LOG2E * HEAD_DIM ** -0.5)).astype(BF16)
    k_ref[...] = jnp.dot(h, wk_ref[...], preferred_element_type=F32).astype(BF16)
    v_ref[...] = jnp.dot(h, wv_ref[...], preferred_element_type=F32).astype(BF16)


def _qkv(x2d, g, wq, wk, wv):
    N, D = x2d.shape
    DA = wq.shape[1]
    TM = ROW_TILE
    wspec = pl.BlockSpec((D, DA), lambda i: (0, 0))
    ospec = pl.BlockSpec((TM, DA), lambda i: (i, 0))
    osh = jax.ShapeDtypeStruct((N, DA), BF16)
    return pl.pallas_call(
        _qkv_kernel,
        out_shape=(osh, osh, osh),
        grid=(N // TM,),
        in_specs=[pl.BlockSpec((TM, D), lambda i: (i, 0)), pl.BlockSpec((1, D), lambda i: (0, 0)),
                  wspec, wspec, wspec],
        out_specs=(ospec, ospec, ospec),
        compiler_params=_cparams(("arbitrary",)),
        name="attn_qkv",
    )(x2d, g.reshape(1, D), wq.astype(BF16), wk.astype(BF16), wv.astype(BF16))


def _attn_kernel(slope_ref, q_ref, k_ref, v_ref, db_ref, lam_ref, sg_ref, o_ref,
                 qz_ref, s_ref, p_ref, m_ref, l_ref, acc_ref, *, lam_init):
    TQ = q_ref.shape[0]
    TK = ATTN_TK
    hd = pl.program_id(1)
    qi = pl.program_id(2)
    slope = slope_ref[hd]

    q = q_ref[...]
    lane = lax.broadcasted_iota(jnp.int32, q.shape, 1)
    zero = jnp.zeros_like(q)
    qz_ref[0:TQ, :] = jnp.where(lane < HEAD_DIM, q, zero)
    qz_ref[TQ:2 * TQ, :] = jnp.where(lane >= HEAD_DIM, q, zero)
    m_ref[...] = jnp.full(m_ref.shape, -jnp.inf, F32)
    l_ref[...] = jnp.zeros(l_ref.shape, F32)
    acc_ref[...] = jnp.zeros(acc_ref.shape, F32)
    LW = acc_ref.shape[1]
    NC = TK // LW
    kpos = lax.broadcasted_iota(jnp.int32, (1, TK), 1)
    q0 = qi * TQ

    def tile(j, diagonal):
        koff = pl.multiple_of(j * TK, TK)
        s = lax.dot_general(qz_ref[...], k_ref[pl.ds(koff, TK), :], (((1,), (1,)), ((), ())),
                            preferred_element_type=F32)
        if diagonal:
            db = db_ref[qi % (TK // TQ)]
            s = jnp.concatenate([s[0:TQ] + db, s[TQ:2 * TQ] + db], axis=0)
        else:
            s = s + slope * (kpos + (j * TK - q0)).astype(F32)
        s_ref[...] = s
        mx = s[:, 0:LW]
        for c in range(1, NC):
            mx = jnp.maximum(mx, s[:, c * LW:(c + 1) * LW])
        m_old = m_ref[...]
        m_new = jnp.maximum(m_old, jnp.max(mx, axis=-1, keepdims=True))
        alpha = jnp.exp2(m_old - m_new)
        m_ref[...] = m_new
        psum = None
        for c in range(NC):
            p = jnp.exp2(s_ref[:, c * LW:(c + 1) * LW] - m_new)
            p_ref[:, c * LW:(c + 1) * LW] = p.astype(BF16)
            psum = p if psum is None else psum + p
        l_ref[...] = alpha * l_ref[...] + psum
        acc_ref[...] = alpha * acc_ref[...] + jnp.dot(p_ref[...], v_ref[pl.ds(koff, TK), :],
                                                      preferred_element_type=F32)

    def past_tile(j, carry):
        tile(j, False)
        return carry

    n_past = q0 // TK
    lax.fori_loop(0, n_past, past_tile, 0)
    tile(n_past, True)

    lamv = lam_ref[...]
    lam = (jnp.exp(jnp.sum(lamv[0:1] * lamv[1:2], axis=-1, keepdims=True))
           - jnp.exp(jnp.sum(lamv[2:3] * lamv[3:4], axis=-1, keepdims=True)) + lam_init)
    on = acc_ref[...] / jnp.sum(l_ref[...], axis=-1, keepdims=True)
    o = on[0:TQ] - lam * on[TQ:2 * TQ]
    o = _rms(o, sg_ref[...], SUBLN_EPS) * (1.0 - lam_init)
    o_ref[...] = o.astype(o_ref.dtype)


def _attention(q, k, v, lamv, subln_g, lam_init):
    B, S, DA = q.shape
    HW = 2 * HEAD_DIM
    TQ = ATTN_TQ
    TK = ATTN_TK
    assert TK % TQ == 0 and S % TK == 0 and TQ % CHUNK == 0
    n_par = TK // TQ
    slopes = jnp.array([LOG2E * 2.0 ** (-8.0 * (h + 1) / N_HEADS) for h in range(N_HEADS)], F32)
    qr = jnp.arange(TQ, dtype=jnp.int32)[None, :, None]
    kc = jnp.arange(TK, dtype=jnp.int32)[None, None, :] - jnp.arange(n_par, dtype=jnp.int32)[:, None, None] * TQ
    rel = jnp.where(kc <= qr, kc, 2 * qr - kc).astype(F32)
    allowed = jnp.floor_divide(kc, CHUNK) <= qr // CHUNK
    diag_bias = jnp.where(allowed[None], slopes[:, None, None, None] * rel[None], NEG_INF)
    return pl.pallas_call(
        functools.partial(_attn_kernel, lam_init=lam_init),
        out_shape=jax.ShapeDtypeStruct((B, S, DA), BF16),
        grid_spec=pltpu.PrefetchScalarGridSpec(
            num_scalar_prefetch=1,
            grid=(B, N_HEADS, S // TQ),
            in_specs=[
                pl.BlockSpec((None, TQ, HW), lambda b, h, i, sl: (b, i, h)),
                pl.BlockSpec((None, S, HW), lambda b, h, i, sl: (b, 0, h)),
                pl.BlockSpec((None, S, HW), lambda b, h, i, sl: (b, 0, h)),
                pl.BlockSpec((None, n_par, TQ, TK), lambda b, h, i, sl: (h, 0, 0, 0)),
                pl.BlockSpec((4, HEAD_DIM), lambda b, h, i, sl: (0, 0)),
                pl.BlockSpec((1, HW), lambda b, h, i, sl: (0, 0)),
            ],
            out_specs=pl.BlockSpec((None, TQ, HW), lambda b, h, i, sl: (b, i, h)),
            scratch_shapes=[
                pltpu.VMEM((2 * TQ, HW), BF16),
                pltpu.VMEM((2 * TQ, TK), F32),
                pltpu.VMEM((2 * TQ, TK), BF16),
                pltpu.VMEM((2 * TQ, HW), F32),
                pltpu.VMEM((2 * TQ, HW), F32),
                pltpu.VMEM((2 * TQ, HW), F32),
            ],
        ),
        compiler_params=_cparams(("arbitrary", "arbitrary", "arbitrary")),
        name="diff_attention",
    )(slopes, q, k, v, diag_bias, lamv, subln_g.reshape(1, HW))


def _wo_kernel(o_ref, w_ref, x_ref, out_ref):
    out_ref[...] = x_ref[...] + jnp.dot(o_ref[...], w_ref[...], preferred_element_type=F32)


def _wo(o2d, wo, x2d):
    N, D = x2d.shape
    DA = o2d.shape[1]
    TM = ROW_TILE
    return pl.pallas_call(
        _wo_kernel,
        out_shape=jax.ShapeDtypeStruct((N, D), F32),
        grid=(N // TM,),
        in_specs=[pl.BlockSpec((TM, DA), lambda i: (i, 0)), pl.BlockSpec((DA, D), lambda i: (0, 0)),
                  pl.BlockSpec((TM, D), lambda i: (i, 0))],
        out_specs=pl.BlockSpec((TM, D), lambda i: (i, 0)),
        compiler_params=_cparams(("arbitrary",)),
        name="attn_wo",
    )(o2d, wo.astype(BF16), x2d)


def _attn_layer(x, g, wq, wk, wv, wo, lq1, lk1, lq2, lk2, subln_g, layer_idx):
    B, S, D = x.shape
    x2d = x.reshape(B * S, D)
    q, k, v = _qkv(x2d, g, wq, wk, wv)
    DA = q.shape[1]
    lam_init = 0.8 - 0.6 * math.exp(-0.3 * layer_idx)
    lamv = jnp.stack([lq1, lk1, lq2, lk2]).astype(F32)
    o = _attention(q.reshape(B, S, DA), k.reshape(B, S, DA), v.reshape(B, S, DA), lamv, subln_g, lam_init)
    return _wo(o.reshape(B * S, DA), wo, x2d).reshape(B, S, D)


def kernel(x, norm_mix_g, norm_ffn_g, pool_w, pool_scale, attn_wq, attn_wk, attn_wv, attn_wo,
           attn_lq1, attn_lk1, attn_lq2, attn_lk2, attn_subln_g, moe_wg, moe_bg, moe_we, moe_be,
           moe_w1, moe_w3, moe_w2, final_g):
    B, S, D = x.shape
    depth = norm_mix_g.shape[0]
    for i in range(depth):
        j = i // 2
        if i % 2 == 0:
            x = _pool_layer(x, norm_mix_g[i], pool_w[j], pool_scale[j])
        else:
            x = _attn_layer(x, norm_mix_g[i], attn_wq[j], attn_wk[j], attn_wv[j], attn_wo[j],
                            attn_lq1[j], attn_lk1[j], attn_lq2[j], attn_lk2[j], attn_subln_g[j], i)
        last = i == depth - 1
        x = _moe_layer(x.reshape(B * S, D), norm_ffn_g[i], moe_wg[i], moe_bg[i], moe_we[i], moe_be[i],
                       moe_w1, moe_w3, moe_w2, i, final_g, last).reshape(B, S, D)
    return x
```

```python
import functools
import math

import jax
import jax.numpy as jnp
from jax import lax
from jax.experimental import pallas as pl
from jax.experimental.pallas import tpu as pltpu

F32 = jnp.float32
BF16 = jnp.bfloat16

RMS_EPS = 1e-6
SUBLN_EPS = 1e-5
POOL_WINDOWS = (2, 4, 8, 16)
CHUNK = 64
N_HEADS = 8
HEAD_DIM = 64
N_GROUPS = 4
EXPERTS_PER_GROUP = 8
N_EXPERTS = N_GROUPS * EXPERTS_PER_GROUP
NEG_INF = -1e30
LOG2E = math.log2(math.e)

VMEM_LIMIT_BYTES = 48 * 1024 * 1024
POOL_TILE = 256
ROW_TILE = 512
EXPERT_BLOCK = 512
ATTN_TQ = 512
ATTN_TK = 512
LANES = 128
LOGIT_ROWS = 40


def _cparams(sem):
    return pltpu.CompilerParams(dimension_semantics=sem, vmem_limit_bytes=VMEM_LIMIT_BYTES)


def _rms(x, g, eps):
    return x * lax.rsqrt(jnp.mean(x * x, axis=-1, keepdims=True) + eps) * g


def _pool_kernel(x_ref, g_ref, band_ref, icnt_ref, w_ref, sc_ref, o_ref, hh_ref):
    T = POOL_TILE
    si = pl.program_id(1)

    @pl.when(si == 0)
    def _():
        hh_ref[0:T, :] = jnp.zeros((T, hh_ref.shape[1]), BF16)

    @pl.when(si > 0)
    def _():
        hh_ref[0:T, :] = hh_ref[T:2 * T, :]

    x = x_ref[...]
    h = _rms(x, g_ref[...], RMS_EPS)
    hh_ref[T:2 * T, :] = h.astype(BF16)
    gw = w_ref.shape[1]
    outs = []
    for gi in range(len(POOL_WINDOWS)):
        cols = slice(gi * gw, (gi + 1) * gw)
        wsum = jnp.dot(band_ref[gi], hh_ref[:, cols], preferred_element_type=F32)
        mix = wsum * icnt_ref[gi] - h[:, cols]
        outs.append(jnp.dot(mix.astype(BF16), w_ref[gi], preferred_element_type=F32))
    o_ref[...] = x + jnp.concatenate(outs, axis=-1) * sc_ref[...]


def _pool_layer(x, g, w, scale):
    B, S, D = x.shape
    T = POOL_TILE
    G = len(POOL_WINDOWS)
    r = jnp.arange(T)[:, None]
    c = jnp.arange(2 * T)[None, :]
    band = jnp.stack([((c <= r + T) & (c > r + T - win)) for win in POOL_WINDOWS]).astype(BF16)
    pos = jnp.arange(S, dtype=jnp.int32)
    icnt = jnp.stack([1.0 / jnp.minimum(pos + 1, win).astype(F32) for win in POOL_WINDOWS])
    icnt = icnt.reshape(G, S, 1)
    return pl.pallas_call(
        _pool_kernel,
        out_shape=jax.ShapeDtypeStruct((B, S, D), F32),
        grid=(B, S // T),
        in_specs=[
            pl.BlockSpec((None, T, D), lambda b, s: (b, s, 0)),
            pl.BlockSpec((1, D), lambda b, s: (0, 0)),
            pl.BlockSpec((G, T, 2 * T), lambda b, s: (0, 0, 0)),
            pl.BlockSpec((G, T, 1), lambda b, s: (0, s, 0)),
            pl.BlockSpec((G, D // G, D // G), lambda b, s: (0, 0, 0)),
            pl.BlockSpec((1, D), lambda b, s: (0, 0)),
        ],
        out_specs=pl.BlockSpec((None, T, D), lambda b, s: (b, s, 0)),
        scratch_shapes=[pltpu.VMEM((2 * T, D), BF16)],
        compiler_params=_cparams(("arbitrary", "arbitrary")),
        name="pool_mixer",
    )(x, g.reshape(1, D), band, icnt, w.astype(BF16), scale.reshape(1, D))


def _router_kernel(x_ref, g_ref, wt_ref, b_ref, e_ref, gate_ref, cnt_ref):
    i = pl.program_id(0)
    x = x_ref[...]
    h = _rms(x, g_ref[...], RMS_EPS)
    lt = lax.dot_general(wt_ref[...], h, (((1,), (1,)), ((), ())),
                         precision=lax.Precision.HIGHEST, preferred_element_type=F32)
    lt = lt + b_ref[...]
    g0, g1, g2, g3 = lt[0:1], lt[1:2], lt[2:3], lt[3:4]
    gmax = jnp.maximum(jnp.maximum(g0, g1), jnp.maximum(g2, g3))
    gidx = jnp.where(g0 == gmax, 0, jnp.where(g1 == gmax, 1, jnp.where(g2 == gmax, 2, 3)))
    gsum = jnp.exp(g0 - gmax) + jnp.exp(g1 - gmax) + jnp.exp(g2 - gmax) + jnp.exp(g3 - gmax)
    g_p = 1.0 / gsum
    E = EXPERTS_PER_GROUP
    e_in = jnp.where(gidx == 0, lt[8:8 + E],
                     jnp.where(gidx == 1, lt[8 + E:8 + 2 * E],
                               jnp.where(gidx == 2, lt[8 + 2 * E:8 + 3 * E], lt[8 + 3 * E:8 + 4 * E])))
    io = lax.broadcasted_iota(jnp.int32, e_in.shape, 0)
    v0 = jnp.max(e_in, axis=0, keepdims=True)
    i0 = jnp.min(jnp.where(e_in == v0, io, E), axis=0, keepdims=True)
    rest = jnp.where(io == i0, -jnp.inf, e_in)
    v1 = jnp.max(rest, axis=0, keepdims=True)
    i1 = jnp.min(jnp.where(rest == v1, io, E), axis=0, keepdims=True)
    ex = jnp.exp(v1 - v0)
    p0 = 1.0 / (1.0 + ex)
    e0 = gidx * E + i0
    e1 = gidx * E + i1
    e_ref[0:1, :] = e0
    e_ref[1:2, :] = e1
    gate_ref[0:1, :] = g_p * p0
    gate_ref[1:2, :] = g_p * (ex * p0)
    io32 = lax.broadcasted_iota(jnp.int32, (N_EXPERTS, x.shape[0]), 0)
    oh = (io32 == e0).astype(F32) + (io32 == e1).astype(F32)
    tile_cnt = jnp.sum(oh, axis=1, keepdims=True)

    @pl.when(i == 0)
    def _():
        cnt_ref[...] = jnp.zeros(cnt_ref.shape, F32)

    cnt_ref[...] += jnp.broadcast_to(tile_cnt, cnt_ref.shape)


def _router(x2d, g, wg, bg, we, be):
    N, D = x2d.shape
    TM = ROW_TILE
    wt = jnp.zeros((LOGIT_ROWS, D), F32).at[0:N_GROUPS].set(wg.T).at[8:8 + N_EXPERTS].set(we.T)
    bt = jnp.zeros((LOGIT_ROWS, 1), F32).at[0:N_GROUPS, 0].set(bg).at[8:8 + N_EXPERTS, 0].set(be)
    return pl.pallas_call(
        _router_kernel,
        out_shape=(
            jax.ShapeDtypeStruct((2, N), jnp.int32),
            jax.ShapeDtypeStruct((2, N), F32),
            jax.ShapeDtypeStruct((N_EXPERTS, 128), F32),
        ),
        grid=(N // TM,),
        in_specs=[
            pl.BlockSpec((TM, D), lambda i: (i, 0)),
            pl.BlockSpec((1, D), lambda i: (0, 0)),
            pl.BlockSpec((LOGIT_ROWS, D), lambda i: (0, 0)),
            pl.BlockSpec((LOGIT_ROWS, 1), lambda i: (0, 0)),
        ],
        out_specs=(
            pl.BlockSpec((2, TM), lambda i: (0, i)),
            pl.BlockSpec((2, TM), lambda i: (0, i)),
            pl.BlockSpec((N_EXPERTS, 128), lambda i: (0, 0)),
        ),
        compiler_params=_cparams(("arbitrary",)),
        name="moe_router",
    )(x2d, g.reshape(1, D), wt, bt)


def _rank_kernel(e_ref, start_ref, tri_ref, dest_ref, carry_ref):
    i = pl.program_id(0)

    @pl.when(i == 0)
    def _():
        carry_ref[...] = jnp.zeros(carry_ref.shape, F32)

    TM = e_ref.shape[1]
    io32 = lax.broadcasted_iota(jnp.int32, (N_EXPERTS, TM), 0)
    base = carry_ref[...] + start_ref[...]
    for k in range(2):
        oh = io32 == e_ref[k:k + 1, :]
        ohf = oh.astype(F32)
        before = jnp.dot(ohf.astype(BF16), tri_ref[...], preferred_element_type=F32)
        slot = jnp.sum(jnp.where(oh, before + base, 0.0), axis=0, keepdims=True)
        dest_ref[k:k + 1, :] = slot.astype(jnp.int32)
        base = base + jnp.sum(ohf, axis=1, keepdims=True)
    carry_ref[...] = base - start_ref[...]


def _rank(e_t, seg_start):
    _, N = e_t.shape
    TM = ROW_TILE
    tri = (jnp.arange(TM)[:, None] < jnp.arange(TM)[None, :]).astype(BF16)
    return pl.pallas_call(
        _rank_kernel,
        out_shape=jax.ShapeDtypeStruct((2, N), jnp.int32),
        grid=(N // TM,),
        in_specs=[
            pl.BlockSpec((2, TM), lambda i: (0, i)),
            pl.BlockSpec((N_EXPERTS, 1), lambda i: (0, 0)),
            pl.BlockSpec((TM, TM), lambda i: (0, 0)),
        ],
        out_specs=pl.BlockSpec((2, TM), lambda i: (0, i)),
        scratch_shapes=[pltpu.VMEM((N_EXPERTS, 1), F32)],
        compiler_params=_cparams(("arbitrary",)),
        name="moe_rank",
    )(e_t, seg_start.astype(F32).reshape(N_EXPERTS, 1), tri)


def _store_token_tiles(dst_ref, val):
    rows, d = val.shape
    nch = d // LANES
    for c in range(nch):
        dst_ref[pl.ds(c, rows, stride=nch), :] = val[:, c * LANES:(c + 1) * LANES]


def _load_token_tiles(src_ref, rows):
    nch = src_ref.shape[0] // rows
    return jnp.concatenate([src_ref[pl.ds(c, rows, stride=nch), :] for c in range(nch)], axis=1)


def _token_copy(src_ref, src_tok, dst_ref, dst_tok, sem, nch):
    return pltpu.make_async_copy(src_ref.at[pl.ds(pl.multiple_of(src_tok * nch, nch), nch)],
                                 dst_ref.at[pl.ds(pl.multiple_of(dst_tok * nch, nch), nch)], sem)


def _dispatch_kernel(fill_ref, npad_ref, nu_ref, dest_ref, x_ref, g_ref, xs_ref, hs_ref, sem, zsem):
    TM, D = x_ref.shape
    nch = D // LANES
    BLK = EXPERT_BLOCK
    n_blocks = xs_ref.shape[0] // (BLK * nch)
    _store_token_tiles(hs_ref, _rms(x_ref[...], g_ref[...], RMS_EPS))

    def issue(t, carry):
        _token_copy(hs_ref, t, xs_ref, dest_ref[0, 0, t], sem, nch).start()
        _token_copy(hs_ref, t, xs_ref, dest_ref[0, 1, t], sem, nch).start()
        return carry

    lax.fori_loop(0, TM, issue, 0, unroll=8)
    for _ in range(2):
        pltpu.make_async_copy(hs_ref, xs_ref.at[pl.ds(0, TM * nch)], sem).wait()

    @pl.when(pl.program_id(0) == pl.num_programs(0) - 1)
    def _():
        hs_ref[...] = jnp.zeros(hs_ref.shape, hs_ref.dtype)

        def zero_copy(first_slot, n_slots):
            return pltpu.make_async_copy(hs_ref.at[pl.ds(0, n_slots * nch)],
                                         xs_ref.at[pl.ds(pl.multiple_of(first_slot * nch, nch), n_slots * nch)], zsem)

        def pad_copies(act):
            for e in range(N_EXPERTS):
                first = fill_ref[e]
                n = npad_ref[e]
                size = BLK // 2
                while size >= 1:
                    @pl.when((n & size) != 0)
                    def _(first=first, size=size):
                        act(zero_copy(first, size))
                    first = first + (n & size)
                    size //= 2
            for b in range(N_EXPERTS):
                @pl.when(nu_ref[0] + b < n_blocks)
                def _(b=b):
                    act(zero_copy((nu_ref[0] + b) * BLK, BLK))

        pad_copies(lambda c: c.start())
        pad_copies(lambda c: c.wait())


def _dispatch(x2d, g, dest_t, n_slots, fill_start, n_pad, n_used):
    N, D = x2d.shape
    TM = ROW_TILE
    nch = D // LANES
    assert EXPERT_BLOCK <= TM
    dest3 = dest_t.reshape(2, N // TM, TM).transpose(1, 0, 2)
    return pl.pallas_call(
        _dispatch_kernel,
        out_shape=jax.ShapeDtypeStruct((n_slots * nch, LANES), F32),
        grid_spec=pltpu.PrefetchScalarGridSpec(
            num_scalar_prefetch=3,
            grid=(N // TM,),
            in_specs=[
                pl.BlockSpec((1, 2, TM), lambda i, *_: (i, 0, 0), memory_space=pltpu.SMEM),
                pl.BlockSpec((TM, D), lambda i, *_: (i, 0)),
                pl.BlockSpec((1, D), lambda i, *_: (0, 0)),
            ],
            out_specs=pl.BlockSpec(memory_space=pl.ANY),
            scratch_shapes=[pltpu.VMEM((TM * nch, LANES), F32), pltpu.SemaphoreType.DMA,
                            pltpu.SemaphoreType.DMA],
        ),
        compiler_params=_cparams(("arbitrary",)),
        name="moe_dispatch",
    )(fill_start, n_pad, n_used, dest3, x2d, g.reshape(1, D))


def _expert_kernel(be_ref, nu_ref, xs_ref, w1_ref, w3_ref, w2_ref, ys_ref, w1b, w3b, w2b):
    i = pl.program_id(0)
    BLK = EXPERT_BLOCK

    @pl.when(i < nu_ref[0])
    def _():
        prev = be_ref[jnp.maximum(i - 1, 0)]

        @pl.when((i == 0) | (be_ref[i] != prev))
        def _():
            w1b[...] = w1_ref[0].astype(BF16)
            w3b[...] = w3_ref[0].astype(BF16)
            w2b[...] = w2_ref[0].astype(BF16)

        xb = _load_token_tiles(xs_ref, BLK).astype(BF16)
        a = jnp.dot(xb, w1b[...], preferred_element_type=F32)
        b = jnp.dot(xb, w3b[...], preferred_element_type=F32)
        act = a * (1.0 / (1.0 + jnp.exp(-a))) * b
        _store_token_tiles(ys_ref, jnp.dot(act.astype(BF16), w2b[...], preferred_element_type=F32))

    @pl.when(i >= nu_ref[0])
    def _():
        ys_ref[...] = jnp.zeros(ys_ref.shape, F32)


def _experts(xs, block_expert, n_used, w1, w3, w2, layer):
    D, DE = w1.shape[2], w1.shape[3]
    nch = D // LANES
    BLK = EXPERT_BLOCK
    n_blocks = xs.shape[0] // (BLK * nch)

    def xmap(i, be, nu):
        return (jnp.minimum(i, jnp.maximum(nu[0] - 1, 0)), 0)

    def wmap(i, be, nu):
        return (layer, be[i], 0, 0)

    return pl.pallas_call(
        _expert_kernel,
        out_shape=jax.ShapeDtypeStruct(xs.shape, F32),
        grid_spec=pltpu.PrefetchScalarGridSpec(
            num_scalar_prefetch=2,
            grid=(n_blocks,),
            in_specs=[
                pl.BlockSpec((BLK * nch, LANES), xmap),
                pl.BlockSpec((None, 1, D, DE), wmap),
                pl.BlockSpec((None, 1, D, DE), wmap),
                pl.BlockSpec((None, 1, DE, D), wmap),
            ],
            out_specs=pl.BlockSpec((BLK * nch, LANES), lambda i, be, nu: (i, 0)),
            scratch_shapes=[
                pltpu.VMEM((D, DE), BF16),
                pltpu.VMEM((D, DE), BF16),
                pltpu.VMEM((DE, D), BF16),
            ],
        ),
        compiler_params=_cparams(("arbitrary",)),
        name="moe_experts",
    )(block_expert, n_used, xs, w1, w3, w2)


def _combine_kernel(dest0_ref, destn_ref, x_ref, gate_ref, ys_ref, g_ref, o_ref, buf, sem, *, final_norm):
    TM, D = x_ref.shape
    nch = D // LANES
    i = pl.program_id(0)
    slot = i % 2

    def gather(dest_ref, to_slot):
        def issue(t, carry):
            _token_copy(ys_ref, dest_ref[0, 0, t], buf.at[to_slot, 0], t, sem.at[to_slot], nch).start()
            _token_copy(ys_ref, dest_ref[0, 1, t], buf.at[to_slot, 1], t, sem.at[to_slot], nch).start()
            return carry
        lax.fori_loop(0, TM, issue, 0, unroll=8)

    @pl.when(i == 0)
    def _():
        gather(dest0_ref, 0)

    @pl.when(i + 1 < pl.num_programs(0))
    def _():
        gather(destn_ref, 1 - slot)

    for k in range(2):
        pltpu.make_async_copy(ys_ref.at[pl.ds(0, TM * nch)], buf.at[slot, k], sem.at[slot]).wait()
    gate = gate_ref[...]
    y = (x_ref[...] + gate[:, 0:1] * _load_token_tiles(buf.at[slot, 0], TM)
         + gate[:, 1:2] * _load_token_tiles(buf.at[slot, 1], TM))
    if final_norm:
        y = _rms(y, g_ref[...], RMS_EPS)
    o_ref[...] = y


def _combine(x2d, dest_t, gate_t, ys, g, final_norm):
    N, D = x2d.shape
    TM = ROW_TILE
    nch = D // LANES
    n_tiles = N // TM
    dest3 = dest_t.reshape(2, n_tiles, TM).transpose(1, 0, 2)
    gate = gate_t.T
    return pl.pallas_call(
        functools.partial(_combine_kernel, final_norm=final_norm),
        out_shape=jax.ShapeDtypeStruct((N, D), F32),
        grid=(n_tiles,),
        in_specs=[
            pl.BlockSpec((1, 2, TM), lambda i: (0, 0, 0), memory_space=pltpu.SMEM),
            pl.BlockSpec((1, 2, TM), lambda i: (jnp.minimum(i + 1, n_tiles - 1), 0, 0), memory_space=pltpu.SMEM),
            pl.BlockSpec((TM, D), lambda i: (i, 0)),
            pl.BlockSpec((TM, 2), lambda i: (i, 0)),
            pl.BlockSpec(memory_space=pl.ANY),
            pl.BlockSpec((1, D), lambda i: (0, 0)),
        ],
        out_specs=pl.BlockSpec((TM, D), lambda i: (i, 0)),
        scratch_shapes=[pltpu.VMEM((2, 2, TM * nch, LANES), F32), pltpu.SemaphoreType.DMA((2,))],
        compiler_params=_cparams(("arbitrary",)),
        name="moe_combine",
    )(dest3, dest3, x2d, gate, ys, g.reshape(1, D))


def _moe_layer(x2d, norm_g, wg, bg, we, be, w1, w3, w2, layer, out_g, final_norm):
    N, D = x2d.shape
    BLK = EXPERT_BLOCK
    n_slots = 2 * N + N_EXPERTS * BLK
    e_t, gate_t, cnt = _router(x2d, norm_g, wg, bg, we, be)
    counts = cnt[:, 0].astype(jnp.int32)
    padded = ((counts + BLK - 1) // BLK) * BLK
    seg_end = jnp.cumsum(padded)
    seg_start = seg_end - padded
    n_blocks = n_slots // BLK
    block_row0 = jnp.arange(n_blocks, dtype=jnp.int32) * BLK
    block_expert = jnp.minimum(
        jnp.sum((seg_end[None, :] <= block_row0[:, None]).astype(jnp.int32), axis=1), N_EXPERTS - 1)
    n_used = (seg_end[-1:] // BLK).astype(jnp.int32)
    dest_t = _rank(e_t, seg_start)
    xs = _dispatch(x2d, norm_g, dest_t, n_slots, seg_start + counts, padded - counts, n_used)
    ys = _experts(xs, block_expert, n_used, w1, w3, w2, layer)
    return _combine(x2d, dest_t, gate_t, ys, out_g, final_norm)


def _qkv_kernel(x_ref, g_ref, wq_ref, wk_ref, wv_ref, q_ref, k_ref, v_ref):
    h = _rms(x_ref[...], g_ref[...], RMS_EPS).astype(BF16)
    q = jnp.dot(h, wq_ref[...], preferred_element_type=F32)
    q_ref[...] = (q * (LOG2E * HEAD_DIM ** -0.5)).astype(BF16)
    k_ref[...] = jnp.dot(h, wk_ref[...], preferred_element_type=F32).astype(BF16)
    v_ref[...] = jnp.dot(h, wv_ref[...], preferred_element_type=F32).astype(BF16)


def _qkv(x2d, g, wq, wk, wv):
    N, D = x2d.shape
    DA = wq.shape[1]
    TM = ROW_TILE
    wspec = pl.BlockSpec((D, DA), lambda i: (0, 0))
    ospec = pl.BlockSpec((TM, DA), lambda i: (i, 0))
    osh = jax.ShapeDtypeStruct((N, DA), BF16)
    return pl.pallas_call(
        _qkv_kernel,
        out_shape=(osh, osh, osh),
        grid=(N // TM,),
        in_specs=[pl.BlockSpec((TM, D), lambda i: (i, 0)), pl.BlockSpec((1, D), lambda i: (0, 0)),
                  wspec, wspec, wspec],
        out_specs=(ospec, ospec, ospec),
        compiler_params=_cparams(("arbitrary",)),
        name="attn_qkv",
    )(x2d, g.reshape(1, D), wq.astype(BF16), wk.astype(BF16), wv.astype(BF16))


def _attn_kernel(slope_ref, q_ref, k_ref, v_ref, db_ref, lam_ref, sg_ref, o_ref,
                 qz_ref, s_ref, p_ref, m_ref, l_ref, acc_ref, *, lam_init):
    TQ = q_ref.shape[0]
    TK = ATTN_TK
    hd = pl.program_id(1)
    qi = pl.program_id(2)
    slope = slope_ref[hd]

    q = q_ref[...]
    lane = lax.broadcasted_iota(jnp.int32, q.shape, 1)
    zero = jnp.zeros_like(q)
    qz_ref[0:TQ, :] = jnp.where(lane < HEAD_DIM, q, zero)
    qz_ref[TQ:2 * TQ, :] = jnp.where(lane >= HEAD_DIM, q, zero)
    m_ref[...] = jnp.full(m_ref.shape, -jnp.inf, F32)
    l_ref[...] = jnp.zeros(l_ref.shape, F32)
    acc_ref[...] = jnp.zeros(acc_ref.shape, F32)
    LW = acc_ref.shape[1]
    NC = TK // LW
    kpos = lax.broadcasted_iota(jnp.int32, (1, TK), 1)
    q0 = qi * TQ

    def tile(j, diagonal):
        koff = pl.multiple_of(j * TK, TK)
        s = lax.dot_general(qz_ref[...], k_ref[pl.ds(koff, TK), :], (((1,), (1,)), ((), ())),
                            preferred_element_type=F32)
        if diagonal:
            db = db_ref[qi % (TK // TQ)]
            s = jnp.concatenate([s[0:TQ] + db, s[TQ:2 * TQ] + db], axis=0)
        else:
            s = s + slope * (kpos + (j * TK - q0)).astype(F32)
        s_ref[...] = s
        mx = s[:, 0:LW]
        for c in range(1, NC):
            mx = jnp.maximum(mx, s[:, c * LW:(c + 1) * LW])
        m_old = m_ref[...]
        m_new = jnp.maximum(m_old, jnp.max(mx, axis=-1, keepdims=True))
        alpha = jnp.exp2(m_old - m_new)
        m_ref[...] = m_new
        psum = None
        for c in range(NC):
            p = jnp.exp2(s_ref[:, c * LW:(c + 1) * LW] - m_new)
            p_ref[:, c * LW:(c + 1) * LW] = p.astype(BF16)
            psum = p if psum is None else psum + p
        l_ref[...] = alpha * l_ref[...] + psum
        acc_ref[...] = alpha * acc_ref[...] + jnp.dot(p_ref[...], v_ref[pl.ds(koff, TK), :],
                                                      preferred_element_type=F32)

    def past_tile(j, carry):
        tile(j, False)
        return carry

    n_past = q0 // TK
    lax.fori_loop(0, n_past, past_tile, 0)
    tile(n_past, True)

    lamv = lam_ref[...]
    lam = (jnp.exp(jnp.sum(lamv[0:1] * lamv[1:2], axis=-1, keepdims=True))
           - jnp.exp(jnp.sum(lamv[2:3] * lamv[3:4], axis=-1, keepdims=True)) + lam_init)
    on = acc_ref[...] / jnp.sum(l_ref[...], axis=-1, keepdims=True)
    o = on[0:TQ] - lam * on[TQ:2 * TQ]
    o = _rms(o, sg_ref[...], SUBLN_EPS) * (1.0 - lam_init)
    o_ref[...] = o.astype(o_ref.dtype)


def _attention(q, k, v, lamv, subln_g, lam_init):
    B, S, DA = q.shape
    HW = 2 * HEAD_DIM
    TQ = ATTN_TQ
    TK = ATTN_TK
    assert TK % TQ == 0 and S % TK == 0 and TQ % CHUNK == 0
    n_par = TK // TQ
    slopes = jnp.array([LOG2E * 2.0 ** (-8.0 * (h + 1) / N_HEADS) for h in range(N_HEADS)], F32)
    qr = jnp.arange(TQ, dtype=jnp.int32)[None, :, None]
    kc = jnp.arange(TK, dtype=jnp.int32)[None, None, :] - jnp.arange(n_par, dtype=jnp.int32)[:, None, None] * TQ
    rel = jnp.where(kc <= qr, kc, 2 * qr - kc).astype(F32)
    allowed = jnp.floor_divide(kc, CHUNK) <= qr // CHUNK
    diag_bias = jnp.where(allowed[None], slopes[:, None, None, None] * rel[None], NEG_INF)
    return pl.pallas_call(
        functools.partial(_attn_kernel, lam_init=lam_init),
        out_shape=jax.ShapeDtypeStruct((B, S, DA), BF16),
        grid_spec=pltpu.PrefetchScalarGridSpec(
            num_scalar_prefetch=1,
            grid=(B, N_HEADS, S // TQ),
            in_specs=[
                pl.BlockSpec((None, TQ, HW), lambda b, h, i, sl: (b, i, h)),
                pl.BlockSpec((None, S, HW), lambda b, h, i, sl: (b, 0, h)),
                pl.BlockSpec((None, S, HW), lambda b, h, i, sl: (b, 0, h)),
                pl.BlockSpec((None, n_par, TQ, TK), lambda b, h, i, sl: (h, 0, 0, 0)),
                pl.BlockSpec((4, HEAD_DIM), lambda b, h, i, sl: (0, 0)),
                pl.BlockSpec((1, HW), lambda b, h, i, sl: (0, 0)),
            ],
            out_specs=pl.BlockSpec((None, TQ, HW), lambda b, h, i, sl: (b, i, h)),
            scratch_shapes=[
                pltpu.VMEM((2 * TQ, HW), BF16),
                pltpu.VMEM((2 * TQ, TK), F32),
                pltpu.VMEM((2 * TQ, TK), BF16),
                pltpu.VMEM((2 * TQ, HW), F32),
                pltpu.VMEM((2 * TQ, HW), F32),
                pltpu.VMEM((2 * TQ, HW), F32),
            ],
        ),
        compiler_params=_cparams(("arbitrary", "arbitrary", "arbitrary")),
        name="diff_attention",
    )(slopes, q, k, v, diag_bias, lamv, subln_g.reshape(1, HW))


def _wo_kernel(o_ref, w_ref, x_ref, out_ref):
    out_ref[...] = x_ref[...] + jnp.dot(o_ref[...], w_ref[...], preferred_element_type=F32)


def _wo(o2d, wo, x2d):
    N, D = x2d.shape
    DA = o2d.shape[1]
    TM = ROW_TILE
    return pl.pallas_call(
        _wo_kernel,
        out_shape=jax.ShapeDtypeStruct((N, D), F32),
        grid=(N // TM,),
        in_specs=[pl.BlockSpec((TM, DA), lambda i: (i, 0)), pl.BlockSpec((DA, D), lambda i: (0, 0)),
                  pl.BlockSpec((TM, D), lambda i: (i, 0))],
        out_specs=pl.BlockSpec((TM, D), lambda i: (i, 0)),
        compiler_params=_cparams(("arbitrary",)),
        name="attn_wo",
    )(o2d, wo.astype(BF16), x2d)


def _attn_layer(x, g, wq, wk, wv, wo, lq1, lk1, lq2, lk2, subln_g, layer_idx):
    B, S, D = x.shape
    x2d = x.reshape(B * S, D)
    q, k, v = _qkv(x2d, g, wq, wk, wv)
    DA = q.shape[1]
    lam_init = 0.8 - 0.6 * math.exp(-0.3 * layer_idx)
    lamv = jnp.stack([lq1, lk1, lq2, lk2]).astype(F32)
    o = _attention(q.reshape(B, S, DA), k.reshape(B, S, DA), v.reshape(B, S, DA), lamv, subln_g, lam_init)
    return _wo(o.reshape(B * S, DA), wo, x2d).reshape(B, S, D)


def kernel(x, norm_mix_g, norm_ffn_g, pool_w, pool_scale, attn_wq, attn_wk, attn_wv, attn_wo,
           attn_lq1, attn_lk1, attn_lq2, attn_lk2, attn_subln_g, moe_wg, moe_bg, moe_we, moe_be,
           moe_w1, moe_w3, moe_w2, final_g):
    B, S, D = x.shape
    depth = norm_mix_g.shape[0]
    for i in range(depth):
        j = i // 2
        if i % 2 == 0:
            x = _pool_layer(x, norm_mix_g[i], pool_w[j], pool_scale[j])
        else:
            x = _attn_layer(x, norm_mix_g[i], attn_wq[j], attn_wk[j], attn_wv[j], attn_wo[j],
                            attn_lq1[j], attn_lk1[j], attn_lq2[j], attn_lk2[j], attn_subln_g[j], i)
        last = i == depth - 1
        x = _moe_layer(x.reshape(B * S, D), norm_ffn_g[i], moe_wg[i], moe_bg[i], moe_we[i], moe_be[i],
                       moe_w1, moe_w3, moe_w2, i, final_g, last).reshape(B, S, D)
    return x
```

```python
import functools
import math

import jax
import jax.numpy as jnp
from jax import lax
from jax.experimental import pallas as pl
from jax.experimental.pallas import tpu as pltpu

F32 = jnp.float32
BF16 = jnp.bfloat16

RMS_EPS = 1e-6
SUBLN_EPS = 1e-5
POOL_WINDOWS = (2, 4, 8, 16)
CHUNK = 64
N_HEADS = 8
HEAD_DIM = 64
N_GROUPS = 4
EXPERTS_PER_GROUP = 8
N_EXPERTS = N_GROUPS * EXPERTS_PER_GROUP
NEG_INF = -1e30
LOG2E = math.log2(math.e)

VMEM_LIMIT_BYTES = 48 * 1024 * 1024
POOL_TILE = 256
ROW_TILE = 512
EXPERT_BLOCK = 512
ATTN_TQ = 512
ATTN_TK = 512
LANES = 128
LOGIT_ROWS = 40


def _cparams(sem):
    return pltpu.CompilerParams(dimension_semantics=sem, vmem_limit_bytes=VMEM_LIMIT_BYTES)


def _rms(x, g, eps):
    return x * lax.rsqrt(jnp.mean(x * x, axis=-1, keepdims=True) + eps) * g


def _pool_kernel(x_ref, g_ref, band_ref, icnt_ref, w_ref, sc_ref, o_ref, hh_ref):
    T = POOL_TILE
    si = pl.program_id(1)

    @pl.when(si == 0)
    def _():
        hh_ref[0:T, :] = jnp.zeros((T, hh_ref.shape[1]), BF16)

    @pl.when(si > 0)
    def _():
        hh_ref[0:T, :] = hh_ref[T:2 * T, :]

    x = x_ref[...]
    h = _rms(x, g_ref[...], RMS_EPS)
    hh_ref[T:2 * T, :] = h.astype(BF16)
    gw = w_ref.shape[1]
    outs = []
    for gi in range(len(POOL_WINDOWS)):
        cols = slice(gi * gw, (gi + 1) * gw)
        wsum = jnp.dot(band_ref[gi], hh_ref[:, cols], preferred_element_type=F32)
        mix = wsum * icnt_ref[gi] - h[:, cols]
        outs.append(jnp.dot(mix.astype(BF16), w_ref[gi], preferred_element_type=F32))
    o_ref[...] = x + jnp.concatenate(outs, axis=-1) * sc_ref[...]


def _pool_layer(x, g, w, scale):
    B, S, D = x.shape
    T = POOL_TILE
    G = len(POOL_WINDOWS)
    r = jnp.arange(T)[:, None]
    c = jnp.arange(2 * T)[None, :]
    band = jnp.stack([((c <= r + T) & (c > r + T - win)) for win in POOL_WINDOWS]).astype(BF16)
    pos = jnp.arange(S, dtype=jnp.int32)
    icnt = jnp.stack([1.0 / jnp.minimum(pos + 1, win).astype(F32) for win in POOL_WINDOWS])
    icnt = icnt.reshape(G, S, 1)
    return pl.pallas_call(
        _pool_kernel,
        out_shape=jax.ShapeDtypeStruct((B, S, D), F32),
        grid=(B, S // T),
        in_specs=[
            pl.BlockSpec((None, T, D), lambda b, s: (b, s, 0)),
            pl.BlockSpec((1, D), lambda b, s: (0, 0)),
            pl.BlockSpec((G, T, 2 * T), lambda b, s: (0, 0, 0)),
            pl.BlockSpec((G, T, 1), lambda b, s: (0, s, 0)),
            pl.BlockSpec((G, D // G, D // G), lambda b, s: (0, 0, 0)),
            pl.BlockSpec((1, D), lambda b, s: (0, 0)),
        ],
        out_specs=pl.BlockSpec((None, T, D), lambda b, s: (b, s, 0)),
        scratch_shapes=[pltpu.VMEM((2 * T, D), BF16)],
        compiler_params=_cparams(("arbitrary", "arbitrary")),
        name="pool_mixer",
    )(x, g.reshape(1, D), band, icnt, w.astype(BF16), scale.reshape(1, D))


def _router_kernel(x_ref, g_ref, wt_ref, b_ref, e_ref, gate_ref, cnt_ref):
    i = pl.program_id(0)
    x = x_ref[...]
    h = _rms(x, g_ref[...], RMS_EPS)
    lt = lax.dot_general(wt_ref[...], h, (((1,), (1,)), ((), ())),
                         precision=lax.Precision.HIGHEST, preferred_element_type=F32)
    lt = lt + b_ref[...]
    g0, g1, g2, g3 = lt[0:1], lt[1:2], lt[2:3], lt[3:4]
    gmax = jnp.maximum(jnp.maximum(g0, g1), jnp.maximum(g2, g3))
    gidx = jnp.where(g0 == gmax, 0, jnp.where(g1 == gmax, 1, jnp.where(g2 == gmax, 2, 3)))
    gsum = jnp.exp(g0 - gmax) + jnp.exp(g1 - gmax) + jnp.exp(g2 - gmax) + jnp.exp(g3 - gmax)
    g_p = 1.0 / gsum
    E = EXPERTS_PER_GROUP
    e_in = jnp.where(gidx == 0, lt[8:8 + E],
                     jnp.where(gidx == 1, lt[8 + E:8 + 2 * E],
                               jnp.where(gidx == 2, lt[8 + 2 * E:8 + 3 * E], lt[8 + 3 * E:8 + 4 * E])))
    io = lax.broadcasted_iota(jnp.int32, e_in.shape, 0)
    v0 = jnp.max(e_in, axis=0, keepdims=True)
    i0 = jnp.min(jnp.where(e_in == v0, io, E), axis=0, keepdims=True)
    rest = jnp.where(io == i0, -jnp.inf, e_in)
    v1 = jnp.max(rest, axis=0, keepdims=True)
    i1 = jnp.min(jnp.where(rest == v1, io, E), axis=0, keepdims=True)
    ex = jnp.exp(v1 - v0)
    p0 = 1.0 / (1.0 + ex)
    e0 = gidx * E + i0
    e1 = gidx * E + i1
    e_ref[0:1, :] = e0
    e_ref[1:2, :] = e1
    gate_ref[0:1, :] = g_p * p0
    gate_ref[1:2, :] = g_p * (ex * p0)
    io32 = lax.broadcasted_iota(jnp.int32, (N_EXPERTS, x.shape[0]), 0)
    oh = (io32 == e0).astype(F32) + (io32 == e1).astype(F32)
    tile_cnt = jnp.sum(oh, axis=1, keepdims=True)

    @pl.when(i == 0)
    def _():
        cnt_ref[...] = jnp.zeros(cnt_ref.shape, F32)

    cnt_ref[...] += jnp.broadcast_to(tile_cnt, cnt_ref.shape)


def _router(x2d, g, wg, bg, we, be):
    N, D = x2d.shape
    TM = ROW_TILE
    wt = jnp.zeros((LOGIT_ROWS, D), F32).at[0:N_GROUPS].set(wg.T).at[8:8 + N_EXPERTS].set(we.T)
    bt = jnp.zeros((LOGIT_ROWS, 1), F32).at[0:N_GROUPS, 0].set(bg).at[8:8 + N_EXPERTS, 0].set(be)
    return pl.pallas_call(
        _router_kernel,
        out_shape=(
            jax.ShapeDtypeStruct((2, N), jnp.int32),
            jax.ShapeDtypeStruct((2, N), F32),
            jax.ShapeDtypeStruct((N_EXPERTS, 128), F32),
        ),
        grid=(N // TM,),
        in_specs=[
            pl.BlockSpec((TM, D), lambda i: (i, 0)),
            pl.BlockSpec((1, D), lambda i: (0, 0)),
            pl.BlockSpec((LOGIT_ROWS, D), lambda i: (0, 0)),
            pl.BlockSpec((LOGIT_ROWS, 1), lambda i: (0, 0)),
        ],
        out_specs=(
            pl.BlockSpec((2, TM), lambda i: (0, i)),
            pl.BlockSpec((2, TM), lambda i: (0, i)),
            pl.BlockSpec((N_EXPERTS, 128), lambda i: (0, 0)),
        ),
        compiler_params=_cparams(("arbitrary",)),
        name="moe_router",
    )(x2d, g.reshape(1, D), wt, bt)


def _rank_kernel(e_ref, start_ref, tri_ref, low_ref, dest_ref, lpos_ref, rstart_ref, rlen_ref, roff_ref, carry_ref):
    i = pl.program_id(0)

    @pl.when(i == 0)
    def _():
        carry_ref[...] = jnp.zeros(carry_ref.shape, F32)

    TM = e_ref.shape[1]
    io32 = lax.broadcasted_iota(jnp.int32, (N_EXPERTS, TM), 0)
    base = carry_ref[...] + start_ref[...]
    oh = [io32 == e_ref[k:k + 1, :] for k in range(2)]
    ohf = [o.astype(F32) for o in oh]
    before = [jnp.dot(o.astype(BF16), tri_ref[...], preferred_element_type=F32) for o in ohf]
    cnt = [jnp.sum(o, axis=1, keepdims=True) for o in ohf]
    run_len = cnt[0] + cnt[1]
    lanes = rlen_ref.shape[2]
    run_off = jnp.dot(low_ref[...], jnp.broadcast_to(run_len, (N_EXPERTS, lanes)),
                      precision=lax.Precision.HIGHEST, preferred_element_type=F32)
    for k in range(2):
        in_run = before[k] + (cnt[0] if k == 1 else 0.0)
        slot = jnp.sum(jnp.where(oh[k], in_run + base, 0.0), axis=0, keepdims=True)
        lpos = jnp.sum(jnp.where(oh[k], in_run + run_off[:, 0:1], 0.0), axis=0, keepdims=True)
        dest_ref[k:k + 1, :] = slot.astype(jnp.int32)
        lpos_ref[k:k + 1, :] = lpos.astype(jnp.int32)
    rstart_ref[0] = jnp.broadcast_to(base, (N_EXPERTS, lanes)).astype(jnp.int32)
    rlen_ref[0] = jnp.broadcast_to(run_len, (N_EXPERTS, lanes)).astype(jnp.int32)
    roff_ref[0] = run_off.astype(jnp.int32)
    carry_ref[...] = carry_ref[...] + run_len


def _rank(e_t, seg_start):
    _, N = e_t.shape
    TM = ROW_TILE
    n_tiles = N // TM
    tri = (jnp.arange(TM)[:, None] < jnp.arange(TM)[None, :]).astype(BF16)
    low = (jnp.arange(N_EXPERTS)[None, :] < jnp.arange(N_EXPERTS)[:, None]).astype(F32)
    tok = jax.ShapeDtypeStruct((2, N), jnp.int32)
    run = jax.ShapeDtypeStruct((n_tiles, N_EXPERTS, LANES), jnp.int32)
    tok_spec = pl.BlockSpec((2, TM), lambda i: (0, i))
    run_spec = pl.BlockSpec((1, N_EXPERTS, LANES), lambda i: (i, 0, 0))
    dest_t, lpos_t, rstart, rlen, roff = pl.pallas_call(
        _rank_kernel,
        out_shape=(tok, tok, run, run, run),
        grid=(n_tiles,),
        in_specs=[
            pl.BlockSpec((2, TM), lambda i: (0, i)),
            pl.BlockSpec((N_EXPERTS, 1), lambda i: (0, 0)),
            pl.BlockSpec((TM, TM), lambda i: (0, 0)),
            pl.BlockSpec((N_EXPERTS, N_EXPERTS), lambda i: (0, 0)),
        ],
        out_specs=(tok_spec, tok_spec, run_spec, run_spec, run_spec),
        scratch_shapes=[pltpu.VMEM((N_EXPERTS, 1), F32)],
        compiler_params=_cparams(("arbitrary",)),
        name="moe_rank",
    )(e_t, seg_start.astype(F32).reshape(N_EXPERTS, 1), tri, low)
    runs = tuple(r[:, :, 0].reshape(-1) for r in (rstart, rlen, roff))
    return dest_t, lpos_t, runs


def _store_token_tiles(dst_ref, val):
    rows, d = val.shape
    nch = d // LANES
    for c in range(nch):
        dst_ref[pl.ds(c, rows, stride=nch), :] = val[:, c * LANES:(c + 1) * LANES]


def _load_token_tiles(src_ref, rows):
    nch = src_ref.shape[0] // rows
    return jnp.concatenate([src_ref[pl.ds(c, rows, stride=nch), :] for c in range(nch)], axis=1)


def _token_copy(src_ref, src_tok, dst_ref, dst_tok, sem, nch):
    return pltpu.make_async_copy(src_ref.at[pl.ds(pl.multiple_of(src_tok * nch, nch), nch)],
                                 dst_ref.at[pl.ds(pl.multiple_of(dst_tok * nch, nch), nch)], sem)


def _dispatch_kernel(rstart_ref, rlen_ref, roff_ref, fill_ref, npad_ref, nu_ref,
                     lpos_ref, x_ref, g_ref, xs_ref, hs_ref, sem, zsem):
    TM, D = x_ref.shape
    nch = D // LANES
    BLK = EXPERT_BLOCK
    n_blocks = xs_ref.shape[0] // (BLK * nch)
    i = pl.program_id(0)
    h = _rms(x_ref[...], g_ref[...], RMS_EPS).astype(BF16)
    pos = lax.broadcasted_iota(jnp.int32, (2 * TM, TM), 0)
    perm = jnp.where(pos == lpos_ref[0:1, :], 1.0, jnp.where(pos == lpos_ref[1:2, :], 1.0, 0.0)).astype(BF16)
    _store_token_tiles(hs_ref, jnp.dot(perm, h, preferred_element_type=F32))

    def run_copy(first_pos, first_slot, n_tok):
        return pltpu.make_async_copy(
            hs_ref.at[pl.ds(pl.multiple_of(first_pos * nch, nch), n_tok * nch)],
            xs_ref.at[pl.ds(pl.multiple_of(first_slot * nch, nch), n_tok * nch)], sem)

    for e in range(N_EXPERTS):
        n = rlen_ref[i * N_EXPERTS + e]
        first_pos = roff_ref[i * N_EXPERTS + e]
        first_slot = rstart_ref[i * N_EXPERTS + e]
        size = TM
        while size >= 1:
            @pl.when((n & size) != 0)
            def _(first_pos=first_pos, first_slot=first_slot, size=size):
                run_copy(first_pos, first_slot, size).start()
            first_pos = first_pos + (n & size)
            first_slot = first_slot + (n & size)
            size //= 2
    for _ in range(2):
        pltpu.make_async_copy(hs_ref.at[pl.ds(0, TM * nch)], xs_ref.at[pl.ds(0, TM * nch)], sem).wait()

    @pl.when(i == pl.num_programs(0) - 1)
    def _():
        hs_ref[...] = jnp.zeros(hs_ref.shape, hs_ref.dtype)

        def zero_copy(first_slot, n_slots):
            return pltpu.make_async_copy(hs_ref.at[pl.ds(0, n_slots * nch)],
                                         xs_ref.at[pl.ds(pl.multiple_of(first_slot * nch, nch), n_slots * nch)], zsem)

        def pad_copies(act):
            for e in range(N_EXPERTS):
                first = fill_ref[e]
                n = npad_ref[e]
                size = BLK // 2
                while size >= 1:
                    @pl.when((n & size) != 0)
                    def _(first=first, size=size):
                        act(zero_copy(first, size))
                    first = first + (n & size)
                    size //= 2
            for b in range(N_EXPERTS):
                @pl.when(nu_ref[0] + b < n_blocks)
                def _(b=b):
                    act(zero_copy((nu_ref[0] + b) * BLK, BLK))

        pad_copies(lambda c: c.start())
        pad_copies(lambda c: c.wait())


def _dispatch(x2d, g, lpos_t, runs, n_slots, fill_start, n_pad, n_used):
    N, D = x2d.shape
    TM = ROW_TILE
    nch = D // LANES
    assert EXPERT_BLOCK <= 2 * TM
    return pl.pallas_call(
        _dispatch_kernel,
        out_shape=jax.ShapeDtypeStruct((n_slots * nch, LANES), F32),
        grid_spec=pltpu.PrefetchScalarGridSpec(
            num_scalar_prefetch=6,
            grid=(N // TM,),
            in_specs=[
                pl.BlockSpec((2, TM), lambda i, *_: (0, i)),
                pl.BlockSpec((TM, D), lambda i, *_: (i, 0)),
                pl.BlockSpec((1, D), lambda i, *_: (0, 0)),
            ],
            out_specs=pl.BlockSpec(memory_space=pl.ANY),
            scratch_shapes=[pltpu.VMEM((2 * TM * nch, LANES), F32), pltpu.SemaphoreType.DMA,
                            pltpu.SemaphoreType.DMA],
        ),
        compiler_params=_cparams(("arbitrary",)),
        name="moe_dispatch",
    )(*runs, fill_start, n_pad, n_used, lpos_t, x2d, g.reshape(1, D))


def _expert_kernel(be_ref, nu_ref, xs_ref, w1_ref, w3_ref, w2_ref, ys_ref, w13b, w2b):
    i = pl.program_id(0)
    BLK = EXPERT_BLOCK
    DE = w2b.shape[0]

    @pl.when(i < nu_ref[0])
    def _():
        prev = be_ref[jnp.maximum(i - 1, 0)]

        @pl.when((i == 0) | (be_ref[i] != prev))
        def _():
            w13b[:, 0:DE] = w1_ref[0].astype(BF16)
            w13b[:, DE:2 * DE] = w3_ref[0].astype(BF16)
            w2b[...] = w2_ref[0].astype(BF16)

        xb = _load_token_tiles(xs_ref, BLK).astype(BF16)
        ab = jnp.dot(xb, w13b[...], preferred_element_type=F32)
        a = ab[:, 0:DE]
        b = ab[:, DE:2 * DE]
        act = a * (1.0 / (1.0 + jnp.exp(-a))) * b
        _store_token_tiles(ys_ref, jnp.dot(act.astype(BF16), w2b[...], preferred_element_type=F32))

    @pl.when(i >= nu_ref[0])
    def _():
        ys_ref[...] = jnp.zeros(ys_ref.shape, F32)


def _experts(xs, block_expert, n_used, w1, w3, w2, layer):
    D, DE = w1.shape[2], w1.shape[3]
    nch = D // LANES
    BLK = EXPERT_BLOCK
    n_blocks = xs.shape[0] // (BLK * nch)

    def xmap(i, be, nu):
        return (jnp.minimum(i, jnp.maximum(nu[0] - 1, 0)), 0)

    def wmap(i, be, nu):
        return (layer, be[i], 0, 0)

    return pl.pallas_call(
        _expert_kernel,
        out_shape=jax.ShapeDtypeStruct(xs.shape, F32),
        grid_spec=pltpu.PrefetchScalarGridSpec(
            num_scalar_prefetch=2,
            grid=(n_blocks,),
            in_specs=[
                pl.BlockSpec((BLK * nch, LANES), xmap),
                pl.BlockSpec((None, 1, D, DE), wmap),
                pl.BlockSpec((None, 1, D, DE), wmap),
                pl.BlockSpec((None, 1, DE, D), wmap),
            ],
            out_specs=pl.BlockSpec((BLK * nch, LANES), lambda i, be, nu: (i, 0)),
            scratch_shapes=[
                pltpu.VMEM((D, 2 * DE), BF16),
                pltpu.VMEM((DE, D), BF16),
            ],
        ),
        compiler_params=_cparams(("arbitrary",)),
        name="moe_experts",
    )(block_expert, n_used, xs, w1, w3, w2)


def _combine_kernel(dest0_ref, destn_ref, x_ref, gate_ref, ys_ref, g_ref, o_ref, buf, sem, *, final_norm):
    TM, D = x_ref.shape
    nch = D // LANES
    i = pl.program_id(0)
    slot = i % 2

    def gather(dest_ref, to_slot):
        def issue(t, carry):
            _token_copy(ys_ref, dest_ref[0, 0, t], buf.at[to_slot, 0], t, sem.at[to_slot], nch).start()
            _token_copy(ys_ref, dest_ref[0, 1, t], buf.at[to_slot, 1], t, sem.at[to_slot], nch).start()
            return carry
        lax.fori_loop(0, TM, issue, 0, unroll=8)

    @pl.when(i == 0)
    def _():
        gather(dest0_ref, 0)

    @pl.when(i + 1 < pl.num_programs(0))
    def _():
        gather(destn_ref, 1 - slot)

    for k in range(2):
        pltpu.make_async_copy(ys_ref.at[pl.ds(0, TM * nch)], buf.at[slot, k], sem.at[slot]).wait()
    gate = gate_ref[...]
    y = (x_ref[...] + gate[:, 0:1] * _load_token_tiles(buf.at[slot, 0], TM)
         + gate[:, 1:2] * _load_token_tiles(buf.at[slot, 1], TM))
    if final_norm:
        y = _rms(y, g_ref[...], RMS_EPS)
    o_ref[...] = y


def _combine(x2d, dest_t, gate_t, ys, g, final_norm):
    N, D = x2d.shape
    TM = ROW_TILE
    nch = D // LANES
    n_tiles = N // TM
    dest3 = dest_t.reshape(2, n_tiles, TM).transpose(1, 0, 2)
    gate = gate_t.T
    return pl.pallas_call(
        functools.partial(_combine_kernel, final_norm=final_norm),
        out_shape=jax.ShapeDtypeStruct((N, D), F32),
        grid=(n_tiles,),
        in_specs=[
            pl.BlockSpec((1, 2, TM), lambda i: (0, 0, 0), memory_space=pltpu.SMEM),
            pl.BlockSpec((1, 2, TM), lambda i: (jnp.minimum(i + 1, n_tiles - 1), 0, 0), memory_space=pltpu.SMEM),
            pl.BlockSpec((TM, D), lambda i: (i, 0)),
            pl.BlockSpec((TM, 2), lambda i: (i, 0)),
            pl.BlockSpec(memory_space=pl.ANY),
            pl.BlockSpec((1, D), lambda i: (0, 0)),
        ],
        out_specs=pl.BlockSpec((TM, D), lambda i: (i, 0)),
        scratch_shapes=[pltpu.VMEM((2, 2, TM * nch, LANES), F32), pltpu.SemaphoreType.DMA((2,))],
        compiler_params=_cparams(("arbitrary",)),
        name="moe_combine",
    )(dest3, dest3, x2d, gate, ys, g.reshape(1, D))


def _moe_layer(x2d, norm_g, wg, bg, we, be, w1, w3, w2, layer, out_g, final_norm):
    N, D = x2d.shape
    BLK = EXPERT_BLOCK
    n_slots = 2 * N + N_EXPERTS * BLK
    e_t, gate_t, cnt = _router(x2d, norm_g, wg, bg, we, be)
    counts = cnt[:, 0].astype(jnp.int32)
    padded = ((counts + BLK - 1) // BLK) * BLK
    seg_end = jnp.cumsum(padded)
    seg_start = seg_end - padded
    n_blocks = n_slots // BLK
    block_row0 = jnp.arange(n_blocks, dtype=jnp.int32) * BLK
    block_expert = jnp.minimum(
        jnp.sum((seg_end[None, :] <= block_row0[:, None]).astype(jnp.int32), axis=1), N_EXPERTS - 1)
    n_used = (seg_end[-1:] // BLK).astype(jnp.int32)
    dest_t, lpos_t, runs = _rank(e_t, seg_start)
    xs = _dispatch(x2d, norm_g, lpos_t, runs, n_slots, seg_start + counts, padded - counts, n_used)
    ys = _experts(xs, block_expert, n_used, w1, w3, w2, layer)
    return _combine(x2d, dest_t, gate_t, ys, out_g, final_norm)


def _qkv_kernel(x_ref, g_ref, wq_ref, wk_ref, wv_ref, q_ref, k_ref, v_ref):
    h = _rms(x_ref[...], g_ref[...], RMS_EPS).astype(BF16)
    q = jnp.dot(h, wq_ref[...], preferred_element_type=F32)
    q_ref[...] = (q * (LOG2E * HEAD_DIM ** -0.5)).astype(BF16)
    k_ref[...] = jnp.dot(h, wk_ref[...], preferred_element_type=F32).astype(BF16)
    v_ref[...] = jnp.dot(h, wv_ref[...], preferred_element_type=F32).astype(BF16)


def _qkv(x2d, g, wq, wk, wv):
    N, D = x2d.shape
    DA = wq.shape[1]
    TM = ROW_TILE
    wspec = pl.BlockSpec((D, DA), lambda i: (0, 0))
    ospec = pl.BlockSpec((TM, DA), lambda i: (i, 0))
    osh = jax.ShapeDtypeStruct((N, DA), BF16)
    return pl.pallas_call(
        _qkv_kernel,
        out_shape=(osh, osh, osh),
        grid=(N // TM,),
        in_specs=[pl.BlockSpec((TM, D), lambda i: (i, 0)), pl.BlockSpec((1, D), lambda i: (0, 0)),
                  wspec, wspec, wspec],
        out_specs=(ospec, ospec, ospec),
        compiler_params=_cparams(("arbitrary",)),
        name="attn_qkv",
    )(x2d, g.reshape(1, D), wq.astype(BF16), wk.astype(BF16), wv.astype(BF16))


def _attn_kernel(slope_ref, q_ref, k_ref, v_ref, db_ref, lam_ref, sg_ref, o_ref,
                 qz_ref, s_ref, p_ref, m_ref, l_ref, acc_ref, *, lam_init):
    TQ = q_ref.shape[0]
    TK = ATTN_TK
    hd = pl.program_id(1)
    qi = pl.program_id(2)
    slope = slope_ref[hd]

    q = q_ref[...]
    lane = lax.broadcasted_iota(jnp.int32, q.shape, 1)
    zero = jnp.zeros_like(q)
    qz_ref[0:TQ, :] = jnp.where(lane < HEAD_DIM, q, zero)
    qz_ref[TQ:2 * TQ, :] = jnp.where(lane >= HEAD_DIM, q, zero)
    m_ref[...] = jnp.full(m_ref.shape, -jnp.inf, F32)
    l_ref[...] = jnp.zeros(l_ref.shape, F32)
    acc_ref[...] = jnp.zeros(acc_ref.shape, F32)
    LW = acc_ref.shape[1]
    NC = TK // LW
    kpos = lax.broadcasted_iota(jnp.int32, (1, TK), 1)
    q0 = qi * TQ

    def tile(j, diagonal):
        koff = pl.multiple_of(j * TK, TK)
        s = lax.dot_general(qz_ref[...], k_ref[pl.ds(koff, TK), :], (((1,), (1,)), ((), ())),
                            preferred_element_type=F32)
        if diagonal:
            db = db_ref[qi % (TK // TQ)]
            s = jnp.concatenate([s[0:TQ] + db, s[TQ:2 * TQ] + db], axis=0)
        else:
            s = s + slope * (kpos + (j * TK - q0)).astype(F32)
        s_ref[...] = s
        mx = s[:, 0:LW]
        for c in range(1, NC):
            mx = jnp.maximum(mx, s[:, c * LW:(c + 1) * LW])
        m_old = m_ref[...]
        m_new = jnp.maximum(m_old, jnp.max(mx, axis=-1, keepdims=True))
        alpha = jnp.exp2(m_old - m_new)
        m_ref[...] = m_new
        psum = None
        for c in range(NC):
            p = jnp.exp2(s_ref[:, c * LW:(c + 1) * LW] - m_new)
            p_ref[:, c * LW:(c + 1) * LW] = p.astype(BF16)
            psum = p if psum is None else psum + p
        l_ref[...] = alpha * l_ref[...] + psum
        acc_ref[...] = alpha * acc_ref[...] + jnp.dot(p_ref[...], v_ref[pl.ds(koff, TK), :],
                                                      preferred_element_type=F32)

    def past_tile(j, carry):
        tile(j, False)
        return carry

    n_past = q0 // TK
    lax.fori_loop(0, n_past, past_tile, 0)
    tile(n_past, True)

    lamv = lam_ref[...]
    lam = (jnp.exp(jnp.sum(lamv[0:1] * lamv[1:2], axis=-1, keepdims=True))
           - jnp.exp(jnp.sum(lamv[2:3] * lamv[3:4], axis=-1, keepdims=True)) + lam_init)
    on = acc_ref[...] / jnp.sum(l_ref[...], axis=-1, keepdims=True)
    o = on[0:TQ] - lam * on[TQ:2 * TQ]
    o = _rms(o, sg_ref[...], SUBLN_EPS) * (1.0 - lam_init)
    o_ref[...] = o.astype(o_ref.dtype)


def _attention(q, k, v, lamv, subln_g, lam_init):
    B, S, DA = q.shape
    HW = 2 * HEAD_DIM
    TQ = ATTN_TQ
    TK = ATTN_TK
    assert TK % TQ == 0 and S % TK == 0 and TQ % CHUNK == 0
    n_par = TK // TQ
    slopes = jnp.array([LOG2E * 2.0 ** (-8.0 * (h + 1) / N_HEADS) for h in range(N_HEADS)], F32)
    qr = jnp.arange(TQ, dtype=jnp.int32)[None, :, None]
    kc = jnp.arange(TK, dtype=jnp.int32)[None, None, :] - jnp.arange(n_par, dtype=jnp.int32)[:, None, None] * TQ
    rel = jnp.where(kc <= qr, kc, 2 * qr - kc).astype(F32)
    allowed = jnp.floor_divide(kc, CHUNK) <= qr // CHUNK
    diag_bias = jnp.where(allowed[None], slopes[:, None, None, None] * rel[None], NEG_INF)
    return pl.pallas_call(
        functools.partial(_attn_kernel, lam_init=lam_init),
        out_shape=jax.ShapeDtypeStruct((B, S, DA), BF16),
        grid_spec=pltpu.PrefetchScalarGridSpec(
            num_scalar_prefetch=1,
            grid=(B, N_HEADS, S // TQ),
            in_specs=[
                pl.BlockSpec((None, TQ, HW), lambda b, h, i, sl: (b, i, h)),
                pl.BlockSpec((None, S, HW), lambda b, h, i, sl: (b, 0, h)),
                pl.BlockSpec((None, S, HW), lambda b, h, i, sl: (b, 0, h)),
                pl.BlockSpec((None, n_par, TQ, TK), lambda b, h, i, sl: (h, 0, 0, 0)),
                pl.BlockSpec((4, HEAD_DIM), lambda b, h, i, sl: (0, 0)),
                pl.BlockSpec((1, HW), lambda b, h, i, sl: (0, 0)),
            ],
            out_specs=pl.BlockSpec((None, TQ, HW), lambda b, h, i, sl: (b, i, h)),
            scratch_shapes=[
                pltpu.VMEM((2 * TQ, HW), BF16),
                pltpu.VMEM((2 * TQ, TK), F32),
                pltpu.VMEM((2 * TQ, TK), BF16),
                pltpu.VMEM((2 * TQ, HW), F32),
                pltpu.VMEM((2 * TQ, HW), F32),
                pltpu.VMEM((2 * TQ, HW), F32),
            ],
        ),
        compiler_params=_cparams(("arbitrary", "arbitrary", "arbitrary")),
        name="diff_attention",
    )(slopes, q, k, v, diag_bias, lamv, subln_g.reshape(1, HW))


def _wo_kernel(o_ref, w_ref, x_ref, out_ref):
    out_ref[...] = x_ref[...] + jnp.dot(o_ref[...], w_ref[...], preferred_element_type=F32)


def _wo(o2d, wo, x2d):
    N, D = x2d.shape
    DA = o2d.shape[1]
    TM = ROW_TILE
    return pl.pallas_call(
        _wo_kernel,
        out_shape=jax.ShapeDtypeStruct((N, D), F32),
        grid=(N // TM,),
        in_specs=[pl.BlockSpec((TM, DA), lambda i: (i, 0)), pl.BlockSpec((DA, D), lambda i: (0, 0)),
                  pl.BlockSpec((TM, D), lambda i: (i, 0))],
        out_specs=pl.BlockSpec((TM, D), lambda i: (i, 0)),
        compiler_params=_cparams(("arbitrary",)),
        name="attn_wo",
    )(o2d, wo.astype(BF16), x2d)


def _attn_layer(x, g, wq, wk, wv, wo, lq1, lk1, lq2, lk2, subln_g, layer_idx):
    B, S, D = x.shape
    x2d = x.reshape(B * S, D)
    q, k, v = _qkv(x2d, g, wq, wk, wv)
    DA = q.shape[1]
    lam_init = 0.8 - 0.6 * math.exp(-0.3 * layer_idx)
    lamv = jnp.stack([lq1, lk1, lq2, lk2]).astype(F32)
    o = _attention(q.reshape(B, S, DA), k.reshape(B, S, DA), v.reshape(B, S, DA), lamv, subln_g, lam_init)
    return _wo(o.reshape(B * S, DA), wo, x2d).reshape(B, S, D)


def kernel(x, norm_mix_g, norm_ffn_g, pool_w, pool_scale, attn_wq, attn_wk, attn_wv, attn_wo,
           attn_lq1, attn_lk1, attn_lq2, attn_lk2, attn_subln_g, moe_wg, moe_bg, moe_we, moe_be,
           moe_w1, moe_w3, moe_w2, final_g):
    B, S, D = x.shape
    depth = norm_mix_g.shape[0]
    for i in range(depth):
        j = i // 2
        if i % 2 == 0:
            x = _pool_layer(x, norm_mix_g[i], pool_w[j], pool_scale[j])
        else:
            x = _attn_layer(x, norm_mix_g[i], attn_wq[j], attn_wk[j], attn_wv[j], attn_wo[j],
                            attn_lq1[j], attn_lk1[j], attn_lq2[j], attn_lk2[j], attn_subln_g[j], i)
        last = i == depth - 1
        x = _moe_layer(x.reshape(B * S, D), norm_ffn_g[i], moe_wg[i], moe_bg[i], moe_we[i], moe_be[i],
                       moe_w1, moe_w3, moe_w2, i, final_g, last).reshape(B, S, D)
    return x
```

```python
import functools
import math

import jax
import jax.numpy as jnp
from jax import lax
from jax.experimental import pallas as pl
from jax.experimental.pallas import tpu as pltpu

F32 = jnp.float32
BF16 = jnp.bfloat16

RMS_EPS = 1e-6
SUBLN_EPS = 1e-5
POOL_WINDOWS = (2, 4, 8, 16)
CHUNK = 64
N_HEADS = 8
HEAD_DIM = 64
N_GROUPS = 4
EXPERTS_PER_GROUP = 8
N_EXPERTS = N_GROUPS * EXPERTS_PER_GROUP
NEG_INF = -1e30
LOG2E = math.log2(math.e)

VMEM_LIMIT_BYTES = 48 * 1024 * 1024
POOL_TILE = 256
ROW_TILE = 512
EXPERT_BLOCK = 512
DISPATCH_TILES = 2
COMBINE_TILES = 2
ATTN_TQ = 512
ATTN_TK = 512
LANES = 128
LOGIT_ROWS = 40


def _cparams(sem):
    return pltpu.CompilerParams(dimension_semantics=sem, vmem_limit_bytes=VMEM_LIMIT_BYTES)


def _rms(x, g, eps):
    return x * lax.rsqrt(jnp.mean(x * x, axis=-1, keepdims=True) + eps) * g


def _pool_kernel(x_ref, g_ref, band_ref, icnt_ref, w_ref, sc_ref, o_ref, hh_ref):
    T = POOL_TILE
    si = pl.program_id(1)

    @pl.when(si == 0)
    def _():
        hh_ref[0:T, :] = jnp.zeros((T, hh_ref.shape[1]), BF16)

    @pl.when(si > 0)
    def _():
        hh_ref[0:T, :] = hh_ref[T:2 * T, :]

    x = x_ref[...]
    h = _rms(x, g_ref[...], RMS_EPS)
    hh_ref[T:2 * T, :] = h.astype(BF16)
    gw = w_ref.shape[1]
    outs = []
    for gi in range(len(POOL_WINDOWS)):
        cols = slice(gi * gw, (gi + 1) * gw)
        wsum = jnp.dot(band_ref[gi], hh_ref[:, cols], preferred_element_type=F32)
        mix = wsum * icnt_ref[gi] - h[:, cols]
        outs.append(jnp.dot(mix.astype(BF16), w_ref[gi], preferred_element_type=F32))
    o_ref[...] = x + jnp.concatenate(outs, axis=-1) * sc_ref[...]


def _pool_layer(x, g, w, scale):
    B, S, D = x.shape
    T = POOL_TILE
    G = len(POOL_WINDOWS)
    r = jnp.arange(T)[:, None]
    c = jnp.arange(2 * T)[None, :]
    band = jnp.stack([((c <= r + T) & (c > r + T - win)) for win in POOL_WINDOWS]).astype(BF16)
    pos = jnp.arange(S, dtype=jnp.int32)
    icnt = jnp.stack([1.0 / jnp.minimum(pos + 1, win).astype(F32) for win in POOL_WINDOWS])
    icnt = icnt.reshape(G, S, 1)
    return pl.pallas_call(
        _pool_kernel,
        out_shape=jax.ShapeDtypeStruct((B, S, D), F32),
        grid=(B, S // T),
        in_specs=[
            pl.BlockSpec((None, T, D), lambda b, s: (b, s, 0)),
            pl.BlockSpec((1, D), lambda b, s: (0, 0)),
            pl.BlockSpec((G, T, 2 * T), lambda b, s: (0, 0, 0)),
            pl.BlockSpec((G, T, 1), lambda b, s: (0, s, 0)),
            pl.BlockSpec((G, D // G, D // G), lambda b, s: (0, 0, 0)),
            pl.BlockSpec((1, D), lambda b, s: (0, 0)),
        ],
        out_specs=pl.BlockSpec((None, T, D), lambda b, s: (b, s, 0)),
        scratch_shapes=[pltpu.VMEM((2 * T, D), BF16)],
        compiler_params=_cparams(("arbitrary", "arbitrary")),
        name="pool_mixer",
    )(x, g.reshape(1, D), band, icnt, w.astype(BF16), scale.reshape(1, D))


def _router_kernel(x_ref, g_ref, wt_ref, b_ref, e_ref, gate_ref, cnt_ref):
    i = pl.program_id(0)
    x = x_ref[...]
    h = _rms(x, g_ref[...], RMS_EPS)
    lt = lax.dot_general(wt_ref[...], h, (((1,), (1,)), ((), ())),
                         precision=lax.Precision.HIGHEST, preferred_element_type=F32)
    lt = lt + b_ref[...]
    g0, g1, g2, g3 = lt[0:1], lt[1:2], lt[2:3], lt[3:4]
    gmax = jnp.maximum(jnp.maximum(g0, g1), jnp.maximum(g2, g3))
    gidx = jnp.where(g0 == gmax, 0, jnp.where(g1 == gmax, 1, jnp.where(g2 == gmax, 2, 3)))
    gsum = jnp.exp(g0 - gmax) + jnp.exp(g1 - gmax) + jnp.exp(g2 - gmax) + jnp.exp(g3 - gmax)
    g_p = 1.0 / gsum
    E = EXPERTS_PER_GROUP
    e_in = jnp.where(gidx == 0, lt[8:8 + E],
                     jnp.where(gidx == 1, lt[8 + E:8 + 2 * E],
                               jnp.where(gidx == 2, lt[8 + 2 * E:8 + 3 * E], lt[8 + 3 * E:8 + 4 * E])))
    io = lax.broadcasted_iota(jnp.int32, e_in.shape, 0)
    v0 = jnp.max(e_in, axis=0, keepdims=True)
    i0 = jnp.min(jnp.where(e_in == v0, io, E), axis=0, keepdims=True)
    rest = jnp.where(io == i0, -jnp.inf, e_in)
    v1 = jnp.max(rest, axis=0, keepdims=True)
    i1 = jnp.min(jnp.where(rest == v1, io, E), axis=0, keepdims=True)
    ex = jnp.exp(v1 - v0)
    p0 = 1.0 / (1.0 + ex)
    e0 = gidx * E + i0
    e1 = gidx * E + i1
    e_ref[0:1, :] = e0
    e_ref[1:2, :] = e1
    gate_ref[0:1, :] = g_p * p0
    gate_ref[1:2, :] = g_p * (ex * p0)
    io32 = lax.broadcasted_iota(jnp.int32, (N_EXPERTS, x.shape[0]), 0)
    oh = (io32 == e0).astype(F32) + (io32 == e1).astype(F32)
    tile_cnt = jnp.sum(oh, axis=1, keepdims=True)

    @pl.when(i == 0)
    def _():
        cnt_ref[...] = jnp.zeros(cnt_ref.shape, F32)

    cnt_ref[...] += jnp.broadcast_to(tile_cnt, cnt_ref.shape)


def _router(x2d, g, wg, bg, we, be):
    N, D = x2d.shape
    TM = ROW_TILE
    wt = jnp.zeros((LOGIT_ROWS, D), F32).at[0:N_GROUPS].set(wg.T).at[8:8 + N_EXPERTS].set(we.T)
    bt = jnp.zeros((LOGIT_ROWS, 1), F32).at[0:N_GROUPS, 0].set(bg).at[8:8 + N_EXPERTS, 0].set(be)
    return pl.pallas_call(
        _router_kernel,
        out_shape=(
            jax.ShapeDtypeStruct((2, N), jnp.int32),
            jax.ShapeDtypeStruct((2, N), F32),
            jax.ShapeDtypeStruct((N_EXPERTS, 128), F32),
        ),
        grid=(N // TM,),
        in_specs=[
            pl.BlockSpec((TM, D), lambda i: (i, 0)),
            pl.BlockSpec((1, D), lambda i: (0, 0)),
            pl.BlockSpec((LOGIT_ROWS, D), lambda i: (0, 0)),
            pl.BlockSpec((LOGIT_ROWS, 1), lambda i: (0, 0)),
        ],
        out_specs=(
            pl.BlockSpec((2, TM), lambda i: (0, i)),
            pl.BlockSpec((2, TM), lambda i: (0, i)),
            pl.BlockSpec((N_EXPERTS, 128), lambda i: (0, 0)),
        ),
        compiler_params=_cparams(("arbitrary",)),
        name="moe_router",
    )(x2d, g.reshape(1, D), wt, bt)


def _rank_kernel(e_ref, start_ref, tri_ref, low_ref, lpos_ref, rstart_ref, rlen_ref, roff_ref, carry_ref):
    i = pl.program_id(0)

    @pl.when(i == 0)
    def _():
        carry_ref[...] = jnp.zeros(carry_ref.shape, F32)

    TM = e_ref.shape[1]
    io32 = lax.broadcasted_iota(jnp.int32, (N_EXPERTS, TM), 0)
    base = carry_ref[...] + start_ref[...]
    oh = [io32 == e_ref[k:k + 1, :] for k in range(2)]
    ohf = [o.astype(F32) for o in oh]
    before = [jnp.dot(o.astype(BF16), tri_ref[...], preferred_element_type=F32) for o in ohf]
    cnt = [jnp.sum(o, axis=1, keepdims=True) for o in ohf]
    run_len = cnt[0] + cnt[1]
    lanes = rlen_ref.shape[2]
    run_off = jnp.dot(low_ref[...], jnp.broadcast_to(run_len, (N_EXPERTS, lanes)),
                      precision=lax.Precision.HIGHEST, preferred_element_type=F32)
    for k in range(2):
        in_run = before[k] + (cnt[0] if k == 1 else 0.0)
        lpos = jnp.sum(jnp.where(oh[k], in_run + run_off[:, 0:1], 0.0), axis=0, keepdims=True)
        lpos_ref[k:k + 1, :] = lpos.astype(jnp.int32)
    rstart_ref[0] = jnp.broadcast_to(base, (N_EXPERTS, lanes)).astype(jnp.int32)
    rlen_ref[0] = jnp.broadcast_to(run_len, (N_EXPERTS, lanes)).astype(jnp.int32)
    roff_ref[0] = run_off.astype(jnp.int32)
    carry_ref[...] = carry_ref[...] + run_len


def _rank(e_t, seg_start):
    _, N = e_t.shape
    TM = ROW_TILE
    n_tiles = N // TM
    tri = (jnp.arange(TM)[:, None] < jnp.arange(TM)[None, :]).astype(BF16)
    low = (jnp.arange(N_EXPERTS)[None, :] < jnp.arange(N_EXPERTS)[:, None]).astype(F32)
    tok = jax.ShapeDtypeStruct((2, N), jnp.int32)
    run = jax.ShapeDtypeStruct((n_tiles, N_EXPERTS, LANES), jnp.int32)
    tok_spec = pl.BlockSpec((2, TM), lambda i: (0, i))
    run_spec = pl.BlockSpec((1, N_EXPERTS, LANES), lambda i: (i, 0, 0))
    lpos_t, rstart, rlen, roff = pl.pallas_call(
        _rank_kernel,
        out_shape=(tok, run, run, run),
        grid=(n_tiles,),
        in_specs=[
            pl.BlockSpec((2, TM), lambda i: (0, i)),
            pl.BlockSpec((N_EXPERTS, 1), lambda i: (0, 0)),
            pl.BlockSpec((TM, TM), lambda i: (0, 0)),
            pl.BlockSpec((N_EXPERTS, N_EXPERTS), lambda i: (0, 0)),
        ],
        out_specs=(tok_spec, run_spec, run_spec, run_spec),
        scratch_shapes=[pltpu.VMEM((N_EXPERTS, 1), F32)],
        compiler_params=_cparams(("arbitrary",)),
        name="moe_rank",
    )(e_t, seg_start.astype(F32).reshape(N_EXPERTS, 1), tri, low)
    runs = tuple(r[:, :, 0].reshape(-1) for r in (rstart, rlen, roff))
    return lpos_t, runs


def _store_token_tiles(dst_ref, val):
    rows, d = val.shape
    nch = d // LANES
    for c in range(nch):
        dst_ref[pl.ds(c, rows, stride=nch), :] = val[:, c * LANES:(c + 1) * LANES]


def _load_token_tiles(src_ref, rows):
    nch = src_ref.shape[0] // rows
    return jnp.concatenate([src_ref[pl.ds(c, rows, stride=nch), :] for c in range(nch)], axis=1)


def _token_copy(src_ref, src_tok, dst_ref, dst_tok, sem, nch):
    return pltpu.make_async_copy(src_ref.at[pl.ds(pl.multiple_of(src_tok * nch, nch), nch)],
                                 dst_ref.at[pl.ds(pl.multiple_of(dst_tok * nch, nch), nch)], sem)


def _dispatch_kernel(rstart_ref, rlen_ref, roff_ref, fill_ref, npad_ref, nu_ref,
                     lpos_ref, x_ref, g_ref, xs_ref, hs_ref, sem, zsem):
    TM = ROW_TILE
    D = x_ref.shape[1]
    nch = D // LANES
    BLK = EXPERT_BLOCK
    n_blocks = xs_ref.shape[0] // (BLK * nch)
    i = pl.program_id(0)

    def wait_copies(buf):
        for _ in range(2):
            pltpu.make_async_copy(hs_ref.at[buf, pl.ds(0, TM * nch)], xs_ref.at[pl.ds(0, TM * nch)],
                                  sem.at[buf]).wait()

    for buf in range(DISPATCH_TILES):
        rows = slice(buf * TM, (buf + 1) * TM)
        tile = i * DISPATCH_TILES + buf
        h = _rms(x_ref[rows, :], g_ref[...], RMS_EPS).astype(BF16)
        pos = lax.broadcasted_iota(jnp.int32, (2 * TM, TM), 0)
        perm = jnp.where(pos == lpos_ref[0:1, rows], 1.0, jnp.where(pos == lpos_ref[1:2, rows], 1.0, 0.0))
        sorted_rows = jnp.dot(perm.astype(BF16), h, preferred_element_type=F32)

        @pl.when(i >= 1)
        def _(buf=buf):
            wait_copies(buf)

        _store_token_tiles(hs_ref.at[buf], sorted_rows)
        for e in range(N_EXPERTS):
            n = rlen_ref[tile * N_EXPERTS + e]
            first_pos = roff_ref[tile * N_EXPERTS + e]
            first_slot = rstart_ref[tile * N_EXPERTS + e]
            size = TM
            while size >= 1:
                @pl.when((n & size) != 0)
                def _(first_pos=first_pos, first_slot=first_slot, size=size, buf=buf):
                    pltpu.make_async_copy(
                        hs_ref.at[buf, pl.ds(pl.multiple_of(first_pos * nch, nch), size * nch)],
                        xs_ref.at[pl.ds(pl.multiple_of(first_slot * nch, nch), size * nch)], sem.at[buf]).start()
                first_pos = first_pos + (n & size)
                first_slot = first_slot + (n & size)
                size //= 2

    @pl.when(i == pl.num_programs(0) - 1)
    def _():
        for buf in range(DISPATCH_TILES):
            wait_copies(buf)
        hs_ref[0] = jnp.zeros(hs_ref.shape[1:], hs_ref.dtype)

        def zero_copy(first_slot, n_slots):
            return pltpu.make_async_copy(hs_ref.at[0, pl.ds(0, n_slots * nch)],
                                         xs_ref.at[pl.ds(pl.multiple_of(first_slot * nch, nch), n_slots * nch)], zsem)

        def pad_copies(act):
            for e in range(N_EXPERTS):
                first = fill_ref[e]
                n = npad_ref[e]
                size = BLK // 2
                while size >= 1:
                    @pl.when((n & size) != 0)
                    def _(first=first, size=size):
                        act(zero_copy(first, size))
                    first = first + (n & size)
                    size //= 2
            for b in range(N_EXPERTS):
                @pl.when(nu_ref[0] + b < n_blocks)
                def _(b=b):
                    act(zero_copy((nu_ref[0] + b) * BLK, BLK))

        pad_copies(lambda c: c.start())
        pad_copies(lambda c: c.wait())


def _dispatch(x2d, g, lpos_t, runs, n_slots, fill_start, n_pad, n_used):
    N, D = x2d.shape
    TM = ROW_TILE
    nch = D // LANES
    assert EXPERT_BLOCK <= 2 * TM
    T = DISPATCH_TILES
    return pl.pallas_call(
        _dispatch_kernel,
        out_shape=jax.ShapeDtypeStruct((n_slots * nch, LANES), F32),
        grid_spec=pltpu.PrefetchScalarGridSpec(
            num_scalar_prefetch=6,
            grid=(N // (T * TM),),
            in_specs=[
                pl.BlockSpec((2, T * TM), lambda i, *_: (0, i)),
                pl.BlockSpec((T * TM, D), lambda i, *_: (i, 0)),
                pl.BlockSpec((1, D), lambda i, *_: (0, 0)),
            ],
            out_specs=pl.BlockSpec(memory_space=pl.ANY),
            scratch_shapes=[pltpu.VMEM((T, 2 * TM * nch, LANES), F32), pltpu.SemaphoreType.DMA((T,)),
                            pltpu.SemaphoreType.DMA],
        ),
        compiler_params=_cparams(("arbitrary",)),
        name="moe_dispatch",
    )(*runs, fill_start, n_pad, n_used, lpos_t, x2d, g.reshape(1, D))


def _expert_kernel(be_ref, nu_ref, xs_ref, w1_ref, w3_ref, w2_ref, ys_ref, w13b, w2b):
    i = pl.program_id(0)
    BLK = EXPERT_BLOCK
    DE = w2b.shape[0]

    @pl.when(i < nu_ref[0])
    def _():
        prev = be_ref[jnp.maximum(i - 1, 0)]

        @pl.when((i == 0) | (be_ref[i] != prev))
        def _():
            w13b[:, 0:DE] = w1_ref[0].astype(BF16)
            w13b[:, DE:2 * DE] = w3_ref[0].astype(BF16)
            w2b[...] = w2_ref[0].astype(BF16)

        xb = _load_token_tiles(xs_ref, BLK).astype(BF16)
        ab = jnp.dot(xb, w13b[...], preferred_element_type=F32)
        a = ab[:, 0:DE]
        b = ab[:, DE:2 * DE]
        act = a * (1.0 / (1.0 + jnp.exp(-a))) * b
        _store_token_tiles(ys_ref, jnp.dot(act.astype(BF16), w2b[...], preferred_element_type=F32))

    @pl.when(i >= nu_ref[0])
    def _():
        ys_ref[...] = jnp.zeros(ys_ref.shape, F32)


def _experts(xs, block_expert, n_used, w1, w3, w2, layer):
    D, DE = w1.shape[2], w1.shape[3]
    nch = D // LANES
    BLK = EXPERT_BLOCK
    n_blocks = xs.shape[0] // (BLK * nch)

    def xmap(i, be, nu):
        return (jnp.minimum(i, jnp.maximum(nu[0] - 1, 0)), 0)

    def wmap(i, be, nu):
        return (layer, be[i], 0, 0)

    return pl.pallas_call(
        _expert_kernel,
        out_shape=jax.ShapeDtypeStruct(xs.shape, F32),
        grid_spec=pltpu.PrefetchScalarGridSpec(
            num_scalar_prefetch=2,
            grid=(n_blocks,),
            in_specs=[
                pl.BlockSpec((BLK * nch, LANES), xmap),
                pl.BlockSpec((None, 1, D, DE), wmap),
                pl.BlockSpec((None, 1, D, DE), wmap),
                pl.BlockSpec((None, 1, DE, D), wmap),
            ],
            out_specs=pl.BlockSpec((BLK * nch, LANES), lambda i, be, nu: (i, 0)),
            scratch_shapes=[
                pltpu.VMEM((D, 2 * DE), BF16),
                pltpu.VMEM((DE, D), BF16),
            ],
        ),
        compiler_params=_cparams(("arbitrary",)),
        name="moe_experts",
    )(block_expert, n_used, xs, w1, w3, w2)


def _combine_kernel(rstart_ref, rlen_ref, roff_ref, lpos_ref, x_ref, gate_ref, ys_ref, g_ref, o_ref,
                    sbuf, ybuf, sem, *, final_norm):
    TM = ROW_TILE
    D = x_ref.shape[1]
    nch = D // LANES
    i = pl.program_id(0)

    def gather_runs(tile, buf):
        for e in range(N_EXPERTS):
            n = rlen_ref[tile * N_EXPERTS + e]
            first_pos = roff_ref[tile * N_EXPERTS + e]
            first_slot = rstart_ref[tile * N_EXPERTS + e]
            size = TM
            while size >= 1:
                @pl.when((n & size) != 0)
                def _(first_pos=first_pos, first_slot=first_slot, size=size):
                    pltpu.make_async_copy(
                        ys_ref.at[pl.ds(pl.multiple_of(first_slot * nch, nch), size * nch)],
                        sbuf.at[buf, pl.ds(pl.multiple_of(first_pos * nch, nch), size * nch)], sem.at[buf]).start()
                first_pos = first_pos + (n & size)
                first_slot = first_slot + (n & size)
                size //= 2

    @pl.when(i == 0)
    def _():
        for buf in range(COMBINE_TILES):
            gather_runs(buf, buf)

    for buf in range(COMBINE_TILES):
        tile = i * COMBINE_TILES + buf
        rows = slice(buf * TM, (buf + 1) * TM)
        for _ in range(2):
            pltpu.make_async_copy(ys_ref.at[pl.ds(0, TM * nch)], sbuf.at[buf, pl.ds(0, TM * nch)],
                                  sem.at[buf]).wait()

        def to_token_order(t, carry, buf=buf):
            tok = pl.ds(pl.multiple_of(t * nch, nch), nch)
            packed = lpos_ref[buf * TM + t]
            pos0 = packed & 0xFFFF
            pos1 = lax.shift_right_logical(packed, 16)
            ybuf[0, tok, :] = sbuf[buf, pl.ds(pl.multiple_of(pos0 * nch, nch), nch), :]
            ybuf[1, tok, :] = sbuf[buf, pl.ds(pl.multiple_of(pos1 * nch, nch), nch), :]
            return carry

        lax.fori_loop(0, TM, to_token_order, 0, unroll=8)

        @pl.when(i + 1 < pl.num_programs(0))
        def _(tile=tile, buf=buf):
            gather_runs(tile + COMBINE_TILES, buf)

        gate = gate_ref[rows, :]
        y = (x_ref[rows, :] + gate[:, 0:1] * _load_token_tiles(ybuf.at[0], TM)
             + gate[:, 1:2] * _load_token_tiles(ybuf.at[1], TM))
        if final_norm:
            y = _rms(y, g_ref[...], RMS_EPS)
        o_ref[rows, :] = y


def _combine(x2d, lpos_t, gate_t, runs, ys, g, final_norm):
    N, D = x2d.shape
    TM = ROW_TILE
    T = COMBINE_TILES
    nch = D // LANES
    n_steps = N // (T * TM)
    assert 2 * TM <= 1 << 16
    packed_pos = lpos_t[0] | (lpos_t[1] << 16)
    gate = gate_t.T
    return pl.pallas_call(
        functools.partial(_combine_kernel, final_norm=final_norm),
        out_shape=jax.ShapeDtypeStruct((N, D), F32),
        grid_spec=pltpu.PrefetchScalarGridSpec(
            num_scalar_prefetch=3,
            grid=(n_steps,),
            in_specs=[
                pl.BlockSpec((T * TM,), lambda i, *_: (i,), memory_space=pltpu.SMEM),
                pl.BlockSpec((T * TM, D), lambda i, *_: (i, 0)),
                pl.BlockSpec((T * TM, 2), lambda i, *_: (i, 0)),
                pl.BlockSpec(memory_space=pl.ANY),
                pl.BlockSpec((1, D), lambda i, *_: (0, 0)),
            ],
            out_specs=pl.BlockSpec((T * TM, D), lambda i, *_: (i, 0)),
            scratch_shapes=[pltpu.VMEM((T, 2 * TM * nch, LANES), F32), pltpu.VMEM((2, TM * nch, LANES), F32),
                            pltpu.SemaphoreType.DMA((T,))],
        ),
        compiler_params=_cparams(("arbitrary",)),
        name="moe_combine",
    )(*runs, packed_pos, x2d, gate, ys, g.reshape(1, D))


def _moe_layer(x2d, norm_g, wg, bg, we, be, w1, w3, w2, layer, out_g, final_norm):
    N, D = x2d.shape
    BLK = EXPERT_BLOCK
    n_slots = 2 * N + N_EXPERTS * BLK
    e_t, gate_t, cnt = _router(x2d, norm_g, wg, bg, we, be)
    counts = cnt[:, 0].astype(jnp.int32)
    padded = ((counts + BLK - 1) // BLK) * BLK
    seg_end = jnp.cumsum(padded)
    seg_start = seg_end - padded
    n_blocks = n_slots // BLK
    block_row0 = jnp.arange(n_blocks, dtype=jnp.int32) * BLK
    block_expert = jnp.minimum(
        jnp.sum((seg_end[None, :] <= block_row0[:, None]).astype(jnp.int32), axis=1), N_EXPERTS - 1)
    n_used = (seg_end[-1:] // BLK).astype(jnp.int32)
    lpos_t, runs = _rank(e_t, seg_start)
    xs = _dispatch(x2d, norm_g, lpos_t, runs, n_slots, seg_start + counts, padded - counts, n_used)
    ys = _experts(xs, block_expert, n_used, w1, w3, w2, layer)
    return _combine(x2d, lpos_t, gate_t, runs, ys, out_g, final_norm)


def _qkv_kernel(x_ref, g_ref, wq_ref, wk_ref, wv_ref, q_ref, k_ref, v_ref):
    h = _rms(x_ref[...], g_ref[...], RMS_EPS).astype(BF16)
    q = jnp.dot(h, wq_ref[...], preferred_element_type=F32)
    q_ref[...] = (q * (LOG2E * HEAD_DIM ** -0.5)).astype(BF16)
    k_ref[...] = jnp.dot(h, wk_ref[...], preferred_element_type=F32).astype(BF16)
    v_ref[...] = jnp.dot(h, wv_ref[...], preferred_element_type=F32).astype(BF16)


def _qkv(x2d, g, wq, wk, wv):
    N, D = x2d.shape
    DA = wq.shape[1]
    TM = ROW_TILE
    wspec = pl.BlockSpec((D, DA), lambda i: (0, 0))
    ospec = pl.BlockSpec((TM, DA), lambda i: (i, 0))
    osh = jax.ShapeDtypeStruct((N, DA), BF16)
    return pl.pallas_call(
        _qkv_kernel,
        out_shape=(osh, osh, osh),
        grid=(N // TM,),
        in_specs=[pl.BlockSpec((TM, D), lambda i: (i, 0)), pl.BlockSpec((1, D), lambda i: (0, 0)),
                  wspec, wspec, wspec],
        out_specs=(ospec, ospec, ospec),
        compiler_params=_cparams(("arbitrary",)),
        name="attn_qkv",
    )(x2d, g.reshape(1, D), wq.astype(BF16), wk.astype(BF16), wv.astype(BF16))


def _attn_kernel(slope_ref, q_ref, k_ref, v_ref, db_ref, lam_ref, sg_ref, o_ref,
                 qz_ref, s_ref, p_ref, m_ref, l_ref, acc_ref, *, lam_init):
    TQ = q_ref.shape[0]
    TK = ATTN_TK
    hd = pl.program_id(1)
    qi = pl.program_id(2)
    slope = slope_ref[hd]

    q = q_ref[...]
    lane = lax.broadcasted_iota(jnp.int32, q.shape, 1)
    zero = jnp.zeros_like(q)
    qz_ref[0:TQ, :] = jnp.where(lane < HEAD_DIM, q, zero)
    qz_ref[TQ:2 * TQ, :] = jnp.where(lane >= HEAD_DIM, q, zero)
    m_ref[...] = jnp.full(m_ref.shape, -jnp.inf, F32)
    l_ref[...] = jnp.zeros(l_ref.shape, F32)
    acc_ref[...] = jnp.zeros(acc_ref.shape, F32)
    LW = acc_ref.shape[1]
    NC = TK // LW
    kpos = lax.broadcasted_iota(jnp.int32, (1, TK), 1)
    q0 = qi * TQ

    def tile(j, diagonal):
        koff = pl.multiple_of(j * TK, TK)
        s = lax.dot_general(qz_ref[...], k_ref[pl.ds(koff, TK), :], (((1,), (1,)), ((), ())),
                            preferred_element_type=F32)
        if diagonal:
            db = db_ref[qi % (TK // TQ)]
            s = jnp.concatenate([s[0:TQ] + db, s[TQ:2 * TQ] + db], axis=0)
        else:
            s = s + slope * (kpos + (j * TK - q0)).astype(F32)
        s_ref[...] = s
        mx = s[:, 0:LW]
        for c in range(1, NC):
            mx = jnp.maximum(mx, s[:, c * LW:(c + 1) * LW])
        m_old = m_ref[...]
        m_new = jnp.maximum(m_old, jnp.max(mx, axis=-1, keepdims=True))
        alpha = jnp.exp2(m_old - m_new)
        m_ref[...] = m_new
        psum = None
        for c in range(NC):
            p = jnp.exp2(s_ref[:, c * LW:(c + 1) * LW] - m_new)
            p_ref[:, c * LW:(c + 1) * LW] = p.astype(BF16)
            psum = p if psum is None else psum + p
        l_ref[...] = alpha * l_ref[...] + psum
        acc_ref[...] = alpha * acc_ref[...] + jnp.dot(p_ref[...], v_ref[pl.ds(koff, TK), :],
                                                      preferred_element_type=F32)

    def past_tile(j, carry):
        tile(j, False)
        return carry

    n_past = q0 // TK
    lax.fori_loop(0, n_past, past_tile, 0)
    tile(n_past, True)

    lamv = lam_ref[...]
    lam = (jnp.exp(jnp.sum(lamv[0:1] * lamv[1:2], axis=-1, keepdims=True))
           - jnp.exp(jnp.sum(lamv[2:3] * lamv[3:4], axis=-1, keepdims=True)) + lam_init)
    on = acc_ref[...] / jnp.sum(l_ref[...], axis=-1, keepdims=True)
    o = on[0:TQ] - lam * on[TQ:2 * TQ]
    o = _rms(o, sg_ref[...], SUBLN_EPS) * (1.0 - lam_init)
    o_ref[...] = o.astype(o_ref.dtype)


def _attention(q, k, v, lamv, subln_g, lam_init):
    B, S, DA = q.shape
    HW = 2 * HEAD_DIM
    TQ = ATTN_TQ
    TK = ATTN_TK
    assert TK % TQ == 0 and S % TK == 0 and TQ % CHUNK == 0
    n_par = TK // TQ
    slopes = jnp.array([LOG2E * 2.0 ** (-8.0 * (h + 1) / N_HEADS) for h in range(N_HEADS)], F32)
    qr = jnp.arange(TQ, dtype=jnp.int32)[None, :, None]
    kc = jnp.arange(TK, dtype=jnp.int32)[None, None, :] - jnp.arange(n_par, dtype=jnp.int32)[:, None, None] * TQ
    rel = jnp.where(kc <= qr, kc, 2 * qr - kc).astype(F32)
    allowed = jnp.floor_divide(kc, CHUNK) <= qr // CHUNK
    diag_bias = jnp.where(allowed[None], slopes[:, None, None, None] * rel[None], NEG_INF)
    return pl.pallas_call(
        functools.partial(_attn_kernel, lam_init=lam_init),
        out_shape=jax.ShapeDtypeStruct((B, S, DA), BF16),
        grid_spec=pltpu.PrefetchScalarGridSpec(
            num_scalar_prefetch=1,
            grid=(B, N_HEADS, S // TQ),
            in_specs=[
                pl.BlockSpec((None, TQ, HW), lambda b, h, i, sl: (b, i, h)),
                pl.BlockSpec((None, S, HW), lambda b, h, i, sl: (b, 0, h)),
                pl.BlockSpec((None, S, HW), lambda b, h, i, sl: (b, 0, h)),
                pl.BlockSpec((None, n_par, TQ, TK), lambda b, h, i, sl: (h, 0, 0, 0)),
                pl.BlockSpec((4, HEAD_DIM), lambda b, h, i, sl: (0, 0)),
                pl.BlockSpec((1, HW), lambda b, h, i, sl: (0, 0)),
            ],
            out_specs=pl.BlockSpec((None, TQ, HW), lambda b, h, i, sl: (b, i, h)),
            scratch_shapes=[
                pltpu.VMEM((2 * TQ, HW), BF16),
                pltpu.VMEM((2 * TQ, TK), F32),
                pltpu.VMEM((2 * TQ, TK), BF16),
                pltpu.VMEM((2 * TQ, HW), F32),
                pltpu.VMEM((2 * TQ, HW), F32),
                pltpu.VMEM((2 * TQ, HW), F32),
            ],
        ),
        compiler_params=_cparams(("arbitrary", "arbitrary", "arbitrary")),
        name="diff_attention",
    )(slopes, q, k, v, diag_bias, lamv, subln_g.reshape(1, HW))


def _wo_kernel(o_ref, w_ref, x_ref, out_ref):
    out_ref[...] = x_ref[...] + jnp.dot(o_ref[...], w_ref[...], preferred_element_type=F32)


def _wo(o2d, wo, x2d):
    N, D = x2d.shape
    DA = o2d.shape[1]
    TM = ROW_TILE
    return pl.pallas_call(
        _wo_kernel,
        out_shape=jax.ShapeDtypeStruct((N, D), F32),
        grid=(N // TM,),
        in_specs=[pl.BlockSpec((TM, DA), lambda i: (i, 0)), pl.BlockSpec((DA, D), lambda i: (0, 0)),
                  pl.BlockSpec((TM, D), lambda i: (i, 0))],
        out_specs=pl.BlockSpec((TM, D), lambda i: (i, 0)),
        compiler_params=_cparams(("arbitrary",)),
        name="attn_wo",
    )(o2d, wo.astype(BF16), x2d)


def _attn_layer(x, g, wq, wk, wv, wo, lq1, lk1, lq2, lk2, subln_g, layer_idx):
    B, S, D = x.shape
    x2d = x.reshape(B * S, D)
    q, k, v = _qkv(x2d, g, wq, wk, wv)
    DA = q.shape[1]
    lam_init = 0.8 - 0.6 * math.exp(-0.3 * layer_idx)
    lamv = jnp.stack([lq1, lk1, lq2, lk2]).astype(F32)
    o = _attention(q.reshape(B, S, DA), k.reshape(B, S, DA), v.reshape(B, S, DA), lamv, subln_g, lam_init)
    return _wo(o.reshape(B * S, DA), wo, x2d).reshape(B, S, D)


def kernel(x, norm_mix_g, norm_ffn_g, pool_w, pool_scale, attn_wq, attn_wk, attn_wv, attn_wo,
           attn_lq1, attn_lk1, attn_lq2, attn_lk2, attn_subln_g, moe_wg, moe_bg, moe_we, moe_be,
           moe_w1, moe_w3, moe_w2, final_g):
    B, S, D = x.shape
    depth = norm_mix_g.shape[0]
    for i in range(depth):
        j = i // 2
        if i % 2 == 0:
            x = _pool_layer(x, norm_mix_g[i], pool_w[j], pool_scale[j])
        else:
            x = _attn_layer(x, norm_mix_g[i], attn_wq[j], attn_wk[j], attn_wv[j], attn_wo[j],
                            attn_lq1[j], attn_lk1[j], attn_lq2[j], attn_lk2[j], attn_subln_g[j], i)
        last = i == depth - 1
        x = _moe_layer(x.reshape(B * S, D), norm_ffn_g[i], moe_wg[i], moe_bg[i], moe_we[i], moe_be[i],
                       moe_w1, moe_w3, moe_w2, i, final_g, last).reshape(B, S, D)
    return x
```

```python
import functools
import math

import jax
import jax.numpy as jnp
from jax import lax
from jax.experimental import pallas as pl
from jax.experimental.pallas import tpu as pltpu

F32 = jnp.float32
BF16 = jnp.bfloat16

RMS_EPS = 1e-6
SUBLN_EPS = 1e-5
POOL_WINDOWS = (2, 4, 8, 16)
CHUNK = 64
N_HEADS = 8
HEAD_DIM = 64
N_GROUPS = 4
EXPERTS_PER_GROUP = 8
N_EXPERTS = N_GROUPS * EXPERTS_PER_GROUP
NEG_INF = -1e30
LOG2E = math.log2(math.e)

VMEM_LIMIT_BYTES = 48 * 1024 * 1024
POOL_TILE = 256
POOL_SUB = 4
ROW_TILE = 512
EXPERT_BLOCK = 512
DISPATCH_TILES = 2
COMBINE_TILES = 2
ATTN_TQ = 512
ATTN_TK = 512
LANES = 128
LOGIT_ROWS = 40


def _cparams(sem):
    return pltpu.CompilerParams(dimension_semantics=sem, vmem_limit_bytes=VMEM_LIMIT_BYTES)


def _rms(x, g, eps):
    return x * lax.rsqrt(jnp.mean(x * x, axis=-1, keepdims=True) + eps) * g


def _pool_kernel(x_ref, g_ref, band_ref, icnt_ref, w_ref, sc_ref, o_ref, hh_ref):
    T = POOL_TILE
    U = POOL_SUB
    si = pl.program_id(1)

    @pl.when(si == 0)
    def _():
        hh_ref[0:T, :] = jnp.zeros((T, hh_ref.shape[1]), BF16)

    @pl.when(si > 0)
    def _():
        hh_ref[0:T, :] = hh_ref[U * T:(U + 1) * T, :]

    gw = w_ref.shape[1]
    hs = []
    for u in range(U):
        h = _rms(x_ref[u * T:(u + 1) * T, :], g_ref[...], RMS_EPS)
        hh_ref[(u + 1) * T:(u + 2) * T, :] = h.astype(BF16)
        hs.append(h)
    for u in range(U):
        rows = slice(u * T, (u + 1) * T)
        outs = []
        for gi in range(len(POOL_WINDOWS)):
            cols = slice(gi * gw, (gi + 1) * gw)
            wsum = jnp.dot(band_ref[gi], hh_ref[u * T:(u + 2) * T, cols], preferred_element_type=F32)
            mix = wsum * icnt_ref[gi, rows, :] - hs[u][:, cols]
            outs.append(jnp.dot(mix.astype(BF16), w_ref[gi], preferred_element_type=F32))
        o_ref[rows, :] = x_ref[rows, :] + jnp.concatenate(outs, axis=-1) * sc_ref[...]


def _pool_layer(x, g, w, scale):
    B, S, D = x.shape
    T = POOL_TILE
    G = len(POOL_WINDOWS)
    r = jnp.arange(T)[:, None]
    c = jnp.arange(2 * T)[None, :]
    band = jnp.stack([((c <= r + T) & (c > r + T - win)) for win in POOL_WINDOWS]).astype(BF16)
    pos = jnp.arange(S, dtype=jnp.int32)
    icnt = jnp.stack([1.0 / jnp.minimum(pos + 1, win).astype(F32) for win in POOL_WINDOWS])
    icnt = icnt.reshape(G, S, 1)
    U = POOL_SUB
    return pl.pallas_call(
        _pool_kernel,
        out_shape=jax.ShapeDtypeStruct((B, S, D), F32),
        grid=(B, S // (U * T)),
        in_specs=[
            pl.BlockSpec((None, U * T, D), lambda b, s: (b, s, 0)),
            pl.BlockSpec((1, D), lambda b, s: (0, 0)),
            pl.BlockSpec((G, T, 2 * T), lambda b, s: (0, 0, 0)),
            pl.BlockSpec((G, U * T, 1), lambda b, s: (0, s, 0)),
            pl.BlockSpec((G, D // G, D // G), lambda b, s: (0, 0, 0)),
            pl.BlockSpec((1, D), lambda b, s: (0, 0)),
        ],
        out_specs=pl.BlockSpec((None, U * T, D), lambda b, s: (b, s, 0)),
        scratch_shapes=[pltpu.VMEM(((U + 1) * T, D), BF16)],
        compiler_params=_cparams(("arbitrary", "arbitrary")),
        name="pool_mixer",
    )(x, g.reshape(1, D), band, icnt, w.astype(BF16), scale.reshape(1, D))


def _router_kernel(x_ref, g_ref, wt_ref, b_ref, e_ref, gate_ref, cnt_ref):
    i = pl.program_id(0)
    x = x_ref[...]
    h = _rms(x, g_ref[...], RMS_EPS)
    lt = lax.dot_general(wt_ref[...], h, (((1,), (1,)), ((), ())),
                         precision=lax.Precision.HIGHEST, preferred_element_type=F32)
    lt = lt + b_ref[...]
    g0, g1, g2, g3 = lt[0:1], lt[1:2], lt[2:3], lt[3:4]
    gmax = jnp.maximum(jnp.maximum(g0, g1), jnp.maximum(g2, g3))
    gidx = jnp.where(g0 == gmax, 0, jnp.where(g1 == gmax, 1, jnp.where(g2 == gmax, 2, 3)))
    gsum = jnp.exp(g0 - gmax) + jnp.exp(g1 - gmax) + jnp.exp(g2 - gmax) + jnp.exp(g3 - gmax)
    g_p = 1.0 / gsum
    E = EXPERTS_PER_GROUP
    e_in = jnp.where(gidx == 0, lt[8:8 + E],
                     jnp.where(gidx == 1, lt[8 + E:8 + 2 * E],
                               jnp.where(gidx == 2, lt[8 + 2 * E:8 + 3 * E], lt[8 + 3 * E:8 + 4 * E])))
    io = lax.broadcasted_iota(jnp.int32, e_in.shape, 0)
    v0 = jnp.max(e_in, axis=0, keepdims=True)
    i0 = jnp.min(jnp.where(e_in == v0, io, E), axis=0, keepdims=True)
    rest = jnp.where(io == i0, -jnp.inf, e_in)
    v1 = jnp.max(rest, axis=0, keepdims=True)
    i1 = jnp.min(jnp.where(rest == v1, io, E), axis=0, keepdims=True)
    ex = jnp.exp(v1 - v0)
    p0 = 1.0 / (1.0 + ex)
    e0 = gidx * E + i0
    e1 = gidx * E + i1
    e_ref[0:1, :] = e0
    e_ref[1:2, :] = e1
    gate_ref[0:1, :] = g_p * p0
    gate_ref[1:2, :] = g_p * (ex * p0)
    io32 = lax.broadcasted_iota(jnp.int32, (N_EXPERTS, x.shape[0]), 0)
    oh = (io32 == e0).astype(F32) + (io32 == e1).astype(F32)
    tile_cnt = jnp.sum(oh, axis=1, keepdims=True)

    @pl.when(i == 0)
    def _():
        cnt_ref[...] = jnp.zeros(cnt_ref.shape, F32)

    cnt_ref[...] += jnp.broadcast_to(tile_cnt, cnt_ref.shape)


def _router(x2d, g, wg, bg, we, be):
    N, D = x2d.shape
    TM = ROW_TILE
    wt = jnp.zeros((LOGIT_ROWS, D), F32).at[0:N_GROUPS].set(wg.T).at[8:8 + N_EXPERTS].set(we.T)
    bt = jnp.zeros((LOGIT_ROWS, 1), F32).at[0:N_GROUPS, 0].set(bg).at[8:8 + N_EXPERTS, 0].set(be)
    return pl.pallas_call(
        _router_kernel,
        out_shape=(
            jax.ShapeDtypeStruct((2, N), jnp.int32),
            jax.ShapeDtypeStruct((2, N), F32),
            jax.ShapeDtypeStruct((N_EXPERTS, 128), F32),
        ),
        grid=(N // TM,),
        in_specs=[
            pl.BlockSpec((TM, D), lambda i: (i, 0)),
            pl.BlockSpec((1, D), lambda i: (0, 0)),
            pl.BlockSpec((LOGIT_ROWS, D), lambda i: (0, 0)),
            pl.BlockSpec((LOGIT_ROWS, 1), lambda i: (0, 0)),
        ],
        out_specs=(
            pl.BlockSpec((2, TM), lambda i: (0, i)),
            pl.BlockSpec((2, TM), lambda i: (0, i)),
            pl.BlockSpec((N_EXPERTS, 128), lambda i: (0, 0)),
        ),
        compiler_params=_cparams(("arbitrary",)),
        name="moe_router",
    )(x2d, g.reshape(1, D), wt, bt)


def _rank_kernel(e_ref, start_ref, tri_ref, low_ref, lpos_ref, rstart_ref, rlen_ref, roff_ref, carry_ref):
    i = pl.program_id(0)

    @pl.when(i == 0)
    def _():
        carry_ref[...] = jnp.zeros(carry_ref.shape, F32)

    TM = e_ref.shape[1]
    io32 = lax.broadcasted_iota(jnp.int32, (N_EXPERTS, TM), 0)
    base = carry_ref[...] + start_ref[...]
    oh = [io32 == e_ref[k:k + 1, :] for k in range(2)]
    ohf = [o.astype(F32) for o in oh]
    before = [jnp.dot(o.astype(BF16), tri_ref[...], preferred_element_type=F32) for o in ohf]
    cnt = [jnp.sum(o, axis=1, keepdims=True) for o in ohf]
    run_len = cnt[0] + cnt[1]
    lanes = rlen_ref.shape[2]
    run_off = jnp.dot(low_ref[...], jnp.broadcast_to(run_len, (N_EXPERTS, lanes)),
                      precision=lax.Precision.HIGHEST, preferred_element_type=F32)
    for k in range(2):
        in_run = before[k] + (cnt[0] if k == 1 else 0.0)
        lpos = jnp.sum(jnp.where(oh[k], in_run + run_off[:, 0:1], 0.0), axis=0, keepdims=True)
        lpos_ref[k:k + 1, :] = lpos.astype(jnp.int32)
    rstart_ref[0] = jnp.broadcast_to(base, (N_EXPERTS, lanes)).astype(jnp.int32)
    rlen_ref[0] = jnp.broadcast_to(run_len, (N_EXPERTS, lanes)).astype(jnp.int32)
    roff_ref[0] = run_off.astype(jnp.int32)
    carry_ref[...] = carry_ref[...] + run_len


def _rank(e_t, seg_start):
    _, N = e_t.shape
    TM = ROW_TILE
    n_tiles = N // TM
    tri = (jnp.arange(TM)[:, None] < jnp.arange(TM)[None, :]).astype(BF16)
    low = (jnp.arange(N_EXPERTS)[None, :] < jnp.arange(N_EXPERTS)[:, None]).astype(F32)
    tok = jax.ShapeDtypeStruct((2, N), jnp.int32)
    run = jax.ShapeDtypeStruct((n_tiles, N_EXPERTS, LANES), jnp.int32)
    tok_spec = pl.BlockSpec((2, TM), lambda i: (0, i))
    run_spec = pl.BlockSpec((1, N_EXPERTS, LANES), lambda i: (i, 0, 0))
    lpos_t, rstart, rlen, roff = pl.pallas_call(
        _rank_kernel,
        out_shape=(tok, run, run, run),
        grid=(n_tiles,),
        in_specs=[
            pl.BlockSpec((2, TM), lambda i: (0, i)),
            pl.BlockSpec((N_EXPERTS, 1), lambda i: (0, 0)),
            pl.BlockSpec((TM, TM), lambda i: (0, 0)),
            pl.BlockSpec((N_EXPERTS, N_EXPERTS), lambda i: (0, 0)),
        ],
        out_specs=(tok_spec, run_spec, run_spec, run_spec),
        scratch_shapes=[pltpu.VMEM((N_EXPERTS, 1), F32)],
        compiler_params=_cparams(("arbitrary",)),
        name="moe_rank",
    )(e_t, seg_start.astype(F32).reshape(N_EXPERTS, 1), tri, low)
    runs = tuple(r[:, :, 0].reshape(-1) for r in (rstart, rlen, roff))
    return lpos_t, runs


def _store_token_tiles(dst_ref, val):
    rows, d = val.shape
    nch = d // LANES
    for c in range(nch):
        dst_ref[pl.ds(c, rows, stride=nch), :] = val[:, c * LANES:(c + 1) * LANES]


def _load_token_tiles(src_ref, rows):
    nch = src_ref.shape[0] // rows
    return jnp.concatenate([src_ref[pl.ds(c, rows, stride=nch), :] for c in range(nch)], axis=1)


def _for_each_run_piece(rstart_ref, rlen_ref, roff_ref, tile, max_len, emit):
    for e in range(N_EXPERTS):
        n = rlen_ref[tile * N_EXPERTS + e]
        first_pos = roff_ref[tile * N_EXPERTS + e]
        first_slot = rstart_ref[tile * N_EXPERTS + e]
        size = max_len
        while size >= 1:
            @pl.when((n & size) != 0)
            def _(first_pos=first_pos, first_slot=first_slot, size=size):
                emit(first_pos, first_slot, size)
            first_pos = first_pos + (n & size)
            first_slot = first_slot + (n & size)
            size //= 2


def _dispatch_kernel(rstart_ref, rlen_ref, roff_ref, fill_ref, npad_ref, nu_ref,
                     lpos_ref, x_ref, g_ref, xs_ref, hs_ref, sem, zsem):
    TM = ROW_TILE
    D = x_ref.shape[1]
    nch = D // LANES
    BLK = EXPERT_BLOCK
    n_blocks = xs_ref.shape[0] // (BLK * nch)
    i = pl.program_id(0)

    def wait_copies(buf):
        for _ in range(2):
            pltpu.make_async_copy(hs_ref.at[buf, pl.ds(0, TM * nch)], xs_ref.at[pl.ds(0, TM * nch)],
                                  sem.at[buf]).wait()

    for buf in range(DISPATCH_TILES):
        rows = slice(buf * TM, (buf + 1) * TM)
        tile = i * DISPATCH_TILES + buf
        h = _rms(x_ref[rows, :], g_ref[...], RMS_EPS).astype(BF16)
        pos = lax.broadcasted_iota(jnp.int32, (2 * TM, TM), 0)
        perm = jnp.where(pos == lpos_ref[0:1, rows], 1.0, jnp.where(pos == lpos_ref[1:2, rows], 1.0, 0.0))
        sorted_rows = jnp.dot(perm.astype(BF16), h, preferred_element_type=F32)

        @pl.when(i >= 1)
        def _(buf=buf):
            wait_copies(buf)

        _store_token_tiles(hs_ref.at[buf], sorted_rows)

        def put_piece(first_pos, first_slot, size, buf=buf):
            pltpu.make_async_copy(
                hs_ref.at[buf, pl.ds(pl.multiple_of(first_pos * nch, nch), size * nch)],
                xs_ref.at[pl.ds(pl.multiple_of(first_slot * nch, nch), size * nch)], sem.at[buf]).start()

        _for_each_run_piece(rstart_ref, rlen_ref, roff_ref, tile, TM, put_piece)

    @pl.when(i == pl.num_programs(0) - 1)
    def _():
        for buf in range(DISPATCH_TILES):
            wait_copies(buf)
        hs_ref[0] = jnp.zeros(hs_ref.shape[1:], hs_ref.dtype)

        def zero_copy(first_slot, n_slots):
            return pltpu.make_async_copy(hs_ref.at[0, pl.ds(0, n_slots * nch)],
                                         xs_ref.at[pl.ds(pl.multiple_of(first_slot * nch, nch), n_slots * nch)], zsem)

        def pad_copies(act):
            for e in range(N_EXPERTS):
                first = fill_ref[e]
                n = npad_ref[e]
                size = BLK // 2
                while size >= 1:
                    @pl.when((n & size) != 0)
                    def _(first=first, size=size):
                        act(zero_copy(first, size))
                    first = first + (n & size)
                    size //= 2
            for b in range(N_EXPERTS):
                @pl.when(nu_ref[0] + b < n_blocks)
                def _(b=b):
                    act(zero_copy((nu_ref[0] + b) * BLK, BLK))

        pad_copies(lambda c: c.start())
        pad_copies(lambda c: c.wait())


def _dispatch(x2d, g, lpos_t, runs, n_slots, fill_start, n_pad, n_used):
    N, D = x2d.shape
    TM = ROW_TILE
    nch = D // LANES
    assert EXPERT_BLOCK <= 2 * TM
    T = DISPATCH_TILES
    return pl.pallas_call(
        _dispatch_kernel,
        out_shape=jax.ShapeDtypeStruct((n_slots * nch, LANES), F32),
        grid_spec=pltpu.PrefetchScalarGridSpec(
            num_scalar_prefetch=6,
            grid=(N // (T * TM),),
            in_specs=[
                pl.BlockSpec((2, T * TM), lambda i, *_: (0, i)),
                pl.BlockSpec((T * TM, D), lambda i, *_: (i, 0)),
                pl.BlockSpec((1, D), lambda i, *_: (0, 0)),
            ],
            out_specs=pl.BlockSpec(memory_space=pl.ANY),
            scratch_shapes=[pltpu.VMEM((T, 2 * TM * nch, LANES), F32), pltpu.SemaphoreType.DMA((T,)),
                            pltpu.SemaphoreType.DMA],
        ),
        compiler_params=_cparams(("arbitrary",)),
        name="moe_dispatch",
    )(*runs, fill_start, n_pad, n_used, lpos_t, x2d, g.reshape(1, D))


def _expert_kernel(be_ref, nu_ref, xs_ref, w1_ref, w3_ref, w2_ref, ys_ref, w13b, w2b):
    i = pl.program_id(0)
    BLK = EXPERT_BLOCK
    DE = w2b.shape[0]

    @pl.when(i < nu_ref[0])
    def _():
        prev = be_ref[jnp.maximum(i - 1, 0)]

        @pl.when((i == 0) | (be_ref[i] != prev))
        def _():
            w13b[:, 0:DE] = w1_ref[0].astype(BF16)
            w13b[:, DE:2 * DE] = w3_ref[0].astype(BF16)
            w2b[...] = w2_ref[0].astype(BF16)

        xb = _load_token_tiles(xs_ref, BLK).astype(BF16)
        ab = jnp.dot(xb, w13b[...], preferred_element_type=F32)
        a = ab[:, 0:DE]
        b = ab[:, DE:2 * DE]
        act = a * (1.0 / (1.0 + jnp.exp(-a))) * b
        _store_token_tiles(ys_ref, jnp.dot(act.astype(BF16), w2b[...], preferred_element_type=F32))

    @pl.when(i >= nu_ref[0])
    def _():
        ys_ref[...] = jnp.zeros(ys_ref.shape, F32)


def _experts(xs, block_expert, n_used, w1, w3, w2, layer):
    D, DE = w1.shape[2], w1.shape[3]
    nch = D // LANES
    BLK = EXPERT_BLOCK
    n_blocks = xs.shape[0] // (BLK * nch)

    def xmap(i, be, nu):
        return (jnp.minimum(i, jnp.maximum(nu[0] - 1, 0)), 0)

    def wmap(i, be, nu):
        return (layer, be[i], 0, 0)

    return pl.pallas_call(
        _expert_kernel,
        out_shape=jax.ShapeDtypeStruct(xs.shape, F32),
        grid_spec=pltpu.PrefetchScalarGridSpec(
            num_scalar_prefetch=2,
            grid=(n_blocks,),
            in_specs=[
                pl.BlockSpec((BLK * nch, LANES), xmap),
                pl.BlockSpec((None, 1, D, DE), wmap),
                pl.BlockSpec((None, 1, D, DE), wmap),
                pl.BlockSpec((None, 1, DE, D), wmap),
            ],
            out_specs=pl.BlockSpec((BLK * nch, LANES), lambda i, be, nu: (i, 0)),
            scratch_shapes=[
                pltpu.VMEM((D, 2 * DE), BF16),
                pltpu.VMEM((DE, D), BF16),
            ],
        ),
        compiler_params=_cparams(("arbitrary",)),
        name="moe_experts",
    )(block_expert, n_used, xs, w1, w3, w2)


def _combine_kernel(rstart_ref, rlen_ref, roff_ref, lpos_ref, x_ref, gate_ref, ys_ref, g_ref, o_ref,
                    sbuf, ybuf, sem, *, final_norm):
    TM = ROW_TILE
    D = x_ref.shape[1]
    nch = D // LANES
    i = pl.program_id(0)

    def gather_runs(tile, buf):
        def get_piece(first_pos, first_slot, size):
            pltpu.make_async_copy(
                ys_ref.at[pl.ds(pl.multiple_of(first_slot * nch, nch), size * nch)],
                sbuf.at[buf, pl.ds(pl.multiple_of(first_pos * nch, nch), size * nch)], sem.at[buf]).start()

        _for_each_run_piece(rstart_ref, rlen_ref, roff_ref, tile, TM, get_piece)

    @pl.when(i == 0)
    def _():
        for buf in range(COMBINE_TILES):
            gather_runs(buf, buf)

    for buf in range(COMBINE_TILES):
        tile = i * COMBINE_TILES + buf
        rows = slice(buf * TM, (buf + 1) * TM)
        for _ in range(2):
            pltpu.make_async_copy(ys_ref.at[pl.ds(0, TM * nch)], sbuf.at[buf, pl.ds(0, TM * nch)],
                                  sem.at[buf]).wait()

        def to_token_order(t, carry, buf=buf):
            tok = pl.ds(pl.multiple_of(t * nch, nch), nch)
            packed = lpos_ref[buf * TM + t]
            pos0 = packed & 0xFFFF
            pos1 = lax.shift_right_logical(packed, 16)
            ybuf[0, tok, :] = sbuf[buf, pl.ds(pl.multiple_of(pos0 * nch, nch), nch), :]
            ybuf[1, tok, :] = sbuf[buf, pl.ds(pl.multiple_of(pos1 * nch, nch), nch), :]
            return carry

        lax.fori_loop(0, TM, to_token_order, 0, unroll=8)

        @pl.when(i + 1 < pl.num_programs(0))
        def _(tile=tile, buf=buf):
            gather_runs(tile + COMBINE_TILES, buf)

        gate = gate_ref[rows, :]
        y = (x_ref[rows, :] + gate[:, 0:1] * _load_token_tiles(ybuf.at[0], TM)
             + gate[:, 1:2] * _load_token_tiles(ybuf.at[1], TM))
        if final_norm:
            y = _rms(y, g_ref[...], RMS_EPS)
        o_ref[rows, :] = y


def _combine(x2d, lpos_t, gate_t, runs, ys, g, final_norm):
    N, D = x2d.shape
    TM = ROW_TILE
    T = COMBINE_TILES
    nch = D // LANES
    n_steps = N // (T * TM)
    assert 2 * TM <= 1 << 16
    packed_pos = lpos_t[0] | (lpos_t[1] << 16)
    gate = gate_t.T
    return pl.pallas_call(
        functools.partial(_combine_kernel, final_norm=final_norm),
        out_shape=jax.ShapeDtypeStruct((N, D), F32),
        grid_spec=pltpu.PrefetchScalarGridSpec(
            num_scalar_prefetch=3,
            grid=(n_steps,),
            in_specs=[
                pl.BlockSpec((T * TM,), lambda i, *_: (i,), memory_space=pltpu.SMEM),
                pl.BlockSpec((T * TM, D), lambda i, *_: (i, 0)),
                pl.BlockSpec((T * TM, 2), lambda i, *_: (i, 0)),
                pl.BlockSpec(memory_space=pl.ANY),
                pl.BlockSpec((1, D), lambda i, *_: (0, 0)),
            ],
            out_specs=pl.BlockSpec((T * TM, D), lambda i, *_: (i, 0)),
            scratch_shapes=[pltpu.VMEM((T, 2 * TM * nch, LANES), F32), pltpu.VMEM((2, TM * nch, LANES), F32),
                            pltpu.SemaphoreType.DMA((T,))],
        ),
        compiler_params=_cparams(("arbitrary",)),
        name="moe_combine",
    )(*runs, packed_pos, x2d, gate, ys, g.reshape(1, D))


def _moe_layer(x2d, norm_g, wg, bg, we, be, w1, w3, w2, layer, out_g, final_norm):
    N, D = x2d.shape
    BLK = EXPERT_BLOCK
    n_slots = 2 * N + N_EXPERTS * BLK
    e_t, gate_t, cnt = _router(x2d, norm_g, wg, bg, we, be)
    counts = cnt[:, 0].astype(jnp.int32)
    padded = ((counts + BLK - 1) // BLK) * BLK
    seg_end = jnp.cumsum(padded)
    seg_start = seg_end - padded
    n_blocks = n_slots // BLK
    block_row0 = jnp.arange(n_blocks, dtype=jnp.int32) * BLK
    block_expert = jnp.minimum(
        jnp.sum((seg_end[None, :] <= block_row0[:, None]).astype(jnp.int32), axis=1), N_EXPERTS - 1)
    n_used = (seg_end[-1:] // BLK).astype(jnp.int32)
    lpos_t, runs = _rank(e_t, seg_start)
    xs = _dispatch(x2d, norm_g, lpos_t, runs, n_slots, seg_start + counts, padded - counts, n_used)
    ys = _experts(xs, block_expert, n_used, w1, w3, w2, layer)
    return _combine(x2d, lpos_t, gate_t, runs, ys, out_g, final_norm)


def _qkv_kernel(x_ref, g_ref, wq_ref, wk_ref, wv_ref, q_ref, k_ref, v_ref):
    h = _rms(x_ref[...], g_ref[...], RMS_EPS).astype(BF16)
    q = jnp.dot(h, wq_ref[...], preferred_element_type=F32)
    q_ref[...] = (q * (LOG2E * HEAD_DIM ** -0.5)).astype(BF16)
    k_ref[...] = jnp.dot(h, wk_ref[...], preferred_element_type=F32).astype(BF16)
    v_ref[...] = jnp.dot(h, wv_ref[...], preferred_element_type=F32).astype(BF16)


def _qkv(x2d, g, wq, wk, wv):
    N, D = x2d.shape
    DA = wq.shape[1]
    TM = ROW_TILE
    wspec = pl.BlockSpec((D, DA), lambda i: (0, 0))
    ospec = pl.BlockSpec((TM, DA), lambda i: (i, 0))
    osh = jax.ShapeDtypeStruct((N, DA), BF16)
    return pl.pallas_call(
        _qkv_kernel,
        out_shape=(osh, osh, osh),
        grid=(N // TM,),
        in_specs=[pl.BlockSpec((TM, D), lambda i: (i, 0)), pl.BlockSpec((1, D), lambda i: (0, 0)),
                  wspec, wspec, wspec],
        out_specs=(ospec, ospec, ospec),
        compiler_params=_cparams(("arbitrary",)),
        name="attn_qkv",
    )(x2d, g.reshape(1, D), wq.astype(BF16), wk.astype(BF16), wv.astype(BF16))


def _attn_kernel(slope_ref, q_ref, k_ref, v_ref, db_ref, lam_ref, sg_ref, o_ref,
                 qz_ref, s_ref, p_ref, m_ref, l_ref, acc_ref, *, lam_init):
    TQ = q_ref.shape[0]
    TK = ATTN_TK
    hd = pl.program_id(1)
    qi = pl.program_id(2)
    slope = slope_ref[hd]

    q = q_ref[...]
    lane = lax.broadcasted_iota(jnp.int32, q.shape, 1)
    zero = jnp.zeros_like(q)
    qz_ref[0:TQ, :] = jnp.where(lane < HEAD_DIM, q, zero)
    qz_ref[TQ:2 * TQ, :] = jnp.where(lane >= HEAD_DIM, q, zero)
    m_ref[...] = jnp.full(m_ref.shape, -jnp.inf, F32)
    l_ref[...] = jnp.zeros(l_ref.shape, F32)
    acc_ref[...] = jnp.zeros(acc_ref.shape, F32)
    LW = acc_ref.shape[1]
    NC = TK // LW
    kpos = lax.broadcasted_iota(jnp.int32, (1, TK), 1)
    q0 = qi * TQ

    def tile(j, diagonal):
        koff = pl.multiple_of(j * TK, TK)
        s = lax.dot_general(qz_ref[...], k_ref[pl.ds(koff, TK), :], (((1,), (1,)), ((), ())),
                            preferred_element_type=F32)
        if diagonal:
            db = db_ref[qi % (TK // TQ)]
            s = jnp.concatenate([s[0:TQ] + db, s[TQ:2 * TQ] + db], axis=0)
        else:
            s = s + slope * (kpos + (j * TK - q0)).astype(F32)
        s_ref[...] = s
        mx = s[:, 0:LW]
        for c in range(1, NC):
            mx = jnp.maximum(mx, s[:, c * LW:(c + 1) * LW])
        m_old = m_ref[...]
        m_new = jnp.maximum(m_old, jnp.max(mx, axis=-1, keepdims=True))
        alpha = jnp.exp2(m_old - m_new)
        m_ref[...] = m_new
        psum = None
        for c in range(NC):
            p = jnp.exp2(s_ref[:, c * LW:(c + 1) * LW] - m_new)
            p_ref[:, c * LW:(c + 1) * LW] = p.astype(BF16)
            psum = p if psum is None else psum + p
        l_ref[...] = alpha * l_ref[...] + psum
        acc_ref[...] = alpha * acc_ref[...] + jnp.dot(p_ref[...], v_ref[pl.ds(koff, TK), :],
                                                      preferred_element_type=F32)

    def past_tile(j, carry):
        tile(j, False)
        return carry

    n_past = q0 // TK
    lax.fori_loop(0, n_past, past_tile, 0)
    tile(n_past, True)

    lamv = lam_ref[...]
    lam = (jnp.exp(jnp.sum(lamv[0:1] * lamv[1:2], axis=-1, keepdims=True))
           - jnp.exp(jnp.sum(lamv[2:3] * lamv[3:4], axis=-1, keepdims=True)) + lam_init)
    on = acc_ref[...] / jnp.sum(l_ref[...], axis=-1, keepdims=True)
    o = on[0:TQ] - lam * on[TQ:2 * TQ]
    o = _rms(o, sg_ref[...], SUBLN_EPS) * (1.0 - lam_init)
    o_ref[...] = o.astype(o_ref.dtype)


def _attention(q, k, v, lamv, subln_g, lam_init):
    B, S, DA = q.shape
    HW = 2 * HEAD_DIM
    TQ = ATTN_TQ
    TK = ATTN_TK
    assert TK % TQ == 0 and S % TK == 0 and TQ % CHUNK == 0
    n_par = TK // TQ
    slopes = jnp.array([LOG2E * 2.0 ** (-8.0 * (h + 1) / N_HEADS) for h in range(N_HEADS)], F32)
    qr = jnp.arange(TQ, dtype=jnp.int32)[None, :, None]
    kc = jnp.arange(TK, dtype=jnp.int32)[None, None, :] - jnp.arange(n_par, dtype=jnp.int32)[:, None, None] * TQ
    rel = jnp.where(kc <= qr, kc, 2 * qr - kc).astype(F32)
    allowed = jnp.floor_divide(kc, CHUNK) <= qr // CHUNK
    diag_bias = jnp.where(allowed[None], slopes[:, None, None, None] * rel[None], NEG_INF)
    return pl.pallas_call(
        functools.partial(_attn_kernel, lam_init=lam_init),
        out_shape=jax.ShapeDtypeStruct((B, S, DA), BF16),
        grid_spec=pltpu.PrefetchScalarGridSpec(
            num_scalar_prefetch=1,
            grid=(B, N_HEADS, S // TQ),
            in_specs=[
                pl.BlockSpec((None, TQ, HW), lambda b, h, i, sl: (b, i, h)),
                pl.BlockSpec((None, S, HW), lambda b, h, i, sl: (b, 0, h)),
                pl.BlockSpec((None, S, HW), lambda b, h, i, sl: (b, 0, h)),
                pl.BlockSpec((None, n_par, TQ, TK), lambda b, h, i, sl: (h, 0, 0, 0)),
                pl.BlockSpec((4, HEAD_DIM), lambda b, h, i, sl: (0, 0)),
                pl.BlockSpec((1, HW), lambda b, h, i, sl: (0, 0)),
            ],
            out_specs=pl.BlockSpec((None, TQ, HW), lambda b, h, i, sl: (b, i, h)),
            scratch_shapes=[
                pltpu.VMEM((2 * TQ, HW), BF16),
                pltpu.VMEM((2 * TQ, TK), F32),
                pltpu.VMEM((2 * TQ, TK), BF16),
                pltpu.VMEM((2 * TQ, HW), F32),
                pltpu.VMEM((2 * TQ, HW), F32),
                pltpu.VMEM((2 * TQ, HW), F32),
            ],
        ),
        compiler_params=_cparams(("arbitrary", "arbitrary", "arbitrary")),
        name="diff_attention",
    )(slopes, q, k, v, diag_bias, lamv, subln_g.reshape(1, HW))


def _wo_kernel(o_ref, w_ref, x_ref, out_ref):
    out_ref[...] = x_ref[...] + jnp.dot(o_ref[...], w_ref[...], preferred_element_type=F32)


def _wo(o2d, wo, x2d):
    N, D = x2d.shape
    DA = o2d.shape[1]
    TM = ROW_TILE
    return pl.pallas_call(
        _wo_kernel,
        out_shape=jax.ShapeDtypeStruct((N, D), F32),
        grid=(N // TM,),
        in_specs=[pl.BlockSpec((TM, DA), lambda i: (i, 0)), pl.BlockSpec((DA, D), lambda i: (0, 0)),
                  pl.BlockSpec((TM, D), lambda i: (i, 0))],
        out_specs=pl.BlockSpec((TM, D), lambda i: (i, 0)),
        compiler_params=_cparams(("arbitrary",)),
        name="attn_wo",
    )(o2d, wo.astype(BF16), x2d)


def _attn_layer(x, g, wq, wk, wv, wo, lq1, lk1, lq2, lk2, subln_g, layer_idx):
    B, S, D = x.shape
    x2d = x.reshape(B * S, D)
    q, k, v = _qkv(x2d, g, wq, wk, wv)
    DA = q.shape[1]
    lam_init = 0.8 - 0.6 * math.exp(-0.3 * layer_idx)
    lamv = jnp.stack([lq1, lk1, lq2, lk2]).astype(F32)
    o = _attention(q.reshape(B, S, DA), k.reshape(B, S, DA), v.reshape(B, S, DA), lamv, subln_g, lam_init)
    return _wo(o.reshape(B * S, DA), wo, x2d).reshape(B, S, D)


def kernel(x, norm_mix_g, norm_ffn_g, pool_w, pool_scale, attn_wq, attn_wk, attn_wv, attn_wo,
           attn_lq1, attn_lk1, attn_lq2, attn_lk2, attn_subln_g, moe_wg, moe_bg, moe_we, moe_be,
           moe_w1, moe_w3, moe_w2, final_g):
    B, S, D = x.shape
    depth = norm_mix_g.shape[0]
    for i in range(depth):
        j = i // 2
        if i % 2 == 0:
            x = _pool_layer(x, norm_mix_g[i], pool_w[j], pool_scale[j])
        else:
            x = _attn_layer(x, norm_mix_g[i], attn_wq[j], attn_wk[j], attn_wv[j], attn_wo[j],
                            attn_lq1[j], attn_lk1[j], attn_lq2[j], attn_lk2[j], attn_subln_g[j], i)
        last = i == depth - 1
        x = _moe_layer(x.reshape(B * S, D), norm_ffn_g[i], moe_wg[i], moe_bg[i], moe_we[i], moe_be[i],
                       moe_w1, moe_w3, moe_w2, i, final_g, last).reshape(B, S, D)
    return x
```

```python
import functools
import math

import jax
import jax.numpy as jnp
from jax import lax
from jax.experimental import pallas as pl
from jax.experimental.pallas import tpu as pltpu

F32 = jnp.float32
BF16 = jnp.bfloat16

RMS_EPS = 1e-6
SUBLN_EPS = 1e-5
POOL_WINDOWS = (2, 4, 8, 16)
CHUNK = 64
N_HEADS = 8
HEAD_DIM = 64
N_GROUPS = 4
EXPERTS_PER_GROUP = 8
N_EXPERTS = N_GROUPS * EXPERTS_PER_GROUP
NEG_INF = -1e30
LOG2E = math.log2(math.e)

VMEM_LIMIT_BYTES = 48 * 1024 * 1024
POOL_TILE = 256
POOL_SUB = 4
ROW_TILE = 512
EXPERT_BLOCK = 512
DISPATCH_TILES = 2
COMBINE_TILES = 2
ATTN_TQ = 512
ATTN_TK = 512
LANES = 128
LOGIT_ROWS = 48


def _cparams(sem):
    return pltpu.CompilerParams(dimension_semantics=sem, vmem_limit_bytes=VMEM_LIMIT_BYTES)


def _rms(x, g, eps):
    return x * lax.rsqrt(jnp.mean(x * x, axis=-1, keepdims=True) + eps) * g


def _pool_kernel(x_ref, g_ref, band_ref, icnt_ref, w_ref, sc_ref, o_ref, hh_ref):
    T = POOL_TILE
    U = POOL_SUB
    si = pl.program_id(1)

    @pl.when(si == 0)
    def _():
        hh_ref[0:T, :] = jnp.zeros((T, hh_ref.shape[1]), BF16)

    @pl.when(si > 0)
    def _():
        hh_ref[0:T, :] = hh_ref[U * T:(U + 1) * T, :]

    gw = w_ref.shape[1]
    hs = []
    for u in range(U):
        h = _rms(x_ref[u * T:(u + 1) * T, :], g_ref[...], RMS_EPS)
        hh_ref[(u + 1) * T:(u + 2) * T, :] = h.astype(BF16)
        hs.append(h)
    for u in range(U):
        rows = slice(u * T, (u + 1) * T)
        outs = []
        for gi in range(len(POOL_WINDOWS)):
            cols = slice(gi * gw, (gi + 1) * gw)
            wsum = jnp.dot(band_ref[gi], hh_ref[u * T:(u + 2) * T, cols], preferred_element_type=F32)
            mix = wsum * icnt_ref[gi, rows, :] - hs[u][:, cols]
            outs.append(jnp.dot(mix.astype(BF16), w_ref[gi], preferred_element_type=F32))
        o_ref[rows, :] = x_ref[rows, :] + jnp.concatenate(outs, axis=-1) * sc_ref[...]


def _pool_layer(x, g, w, scale):
    B, S, D = x.shape
    T = POOL_TILE
    G = len(POOL_WINDOWS)
    r = jnp.arange(T)[:, None]
    c = jnp.arange(2 * T)[None, :]
    band = jnp.stack([((c <= r + T) & (c > r + T - win)) for win in POOL_WINDOWS]).astype(BF16)
    pos = jnp.arange(S, dtype=jnp.int32)
    icnt = jnp.stack([1.0 / jnp.minimum(pos + 1, win).astype(F32) for win in POOL_WINDOWS])
    icnt = icnt.reshape(G, S, 1)
    U = POOL_SUB
    return pl.pallas_call(
        _pool_kernel,
        out_shape=jax.ShapeDtypeStruct((B, S, D), F32),
        grid=(B, S // (U * T)),
        in_specs=[
            pl.BlockSpec((None, U * T, D), lambda b, s: (b, s, 0)),
            pl.BlockSpec((1, D), lambda b, s: (0, 0)),
            pl.BlockSpec((G, T, 2 * T), lambda b, s: (0, 0, 0)),
            pl.BlockSpec((G, U * T, 1), lambda b, s: (0, s, 0)),
            pl.BlockSpec((G, D // G, D // G), lambda b, s: (0, 0, 0)),
            pl.BlockSpec((1, D), lambda b, s: (0, 0)),
        ],
        out_specs=pl.BlockSpec((None, U * T, D), lambda b, s: (b, s, 0)),
        scratch_shapes=[pltpu.VMEM(((U + 1) * T, D), BF16)],
        compiler_params=_cparams(("arbitrary", "arbitrary")),
        name="pool_mixer",
    )(x, g.reshape(1, D), band, icnt, w.astype(BF16), scale.reshape(1, D))


def _router_kernel(x_ref, g_ref, wt_ref, b_ref, e_ref, gate_ref, cnt_ref):
    i = pl.program_id(0)
    x = x_ref[...]
    h = _rms(x, g_ref[...], RMS_EPS)
    h_hi = h.astype(BF16)
    h_lo = (h - h_hi.astype(F32)).astype(BF16)
    nt = (((1,), (1,)), ((), ()))
    both = lax.dot_general(wt_ref[...], h_hi, nt, preferred_element_type=F32)
    cross = lax.dot_general(wt_ref[0:LOGIT_ROWS, :], h_lo, nt, preferred_element_type=F32)
    lt = both[0:LOGIT_ROWS] + both[LOGIT_ROWS:2 * LOGIT_ROWS] + cross + b_ref[...]
    g0, g1, g2, g3 = lt[0:1], lt[1:2], lt[2:3], lt[3:4]
    gmax = jnp.maximum(jnp.maximum(g0, g1), jnp.maximum(g2, g3))
    gidx = jnp.where(g0 == gmax, 0, jnp.where(g1 == gmax, 1, jnp.where(g2 == gmax, 2, 3)))
    gsum = jnp.exp(g0 - gmax) + jnp.exp(g1 - gmax) + jnp.exp(g2 - gmax) + jnp.exp(g3 - gmax)
    g_p = 1.0 / gsum
    E = EXPERTS_PER_GROUP
    e_in = jnp.where(gidx == 0, lt[8:8 + E],
                     jnp.where(gidx == 1, lt[8 + E:8 + 2 * E],
                               jnp.where(gidx == 2, lt[8 + 2 * E:8 + 3 * E], lt[8 + 3 * E:8 + 4 * E])))
    io = lax.broadcasted_iota(jnp.int32, e_in.shape, 0)
    v0 = jnp.max(e_in, axis=0, keepdims=True)
    i0 = jnp.min(jnp.where(e_in == v0, io, E), axis=0, keepdims=True)
    rest = jnp.where(io == i0, -jnp.inf, e_in)
    v1 = jnp.max(rest, axis=0, keepdims=True)
    i1 = jnp.min(jnp.where(rest == v1, io, E), axis=0, keepdims=True)
    ex = jnp.exp(v1 - v0)
    p0 = 1.0 / (1.0 + ex)
    e0 = gidx * E + i0
    e1 = gidx * E + i1
    e_ref[0:1, :] = e0
    e_ref[1:2, :] = e1
    gate_ref[0:1, :] = g_p * p0
    gate_ref[1:2, :] = g_p * (ex * p0)
    io32 = lax.broadcasted_iota(jnp.int32, (N_EXPERTS, x.shape[0]), 0)
    oh = (io32 == e0).astype(F32) + (io32 == e1).astype(F32)
    tile_cnt = jnp.sum(oh, axis=1, keepdims=True)

    @pl.when(i == 0)
    def _():
        cnt_ref[...] = jnp.zeros(cnt_ref.shape, F32)

    cnt_ref[...] += jnp.broadcast_to(tile_cnt, cnt_ref.shape)


def _router(x2d, g, wg, bg, we, be):
    N, D = x2d.shape
    TM = ROW_TILE
    wt = jnp.zeros((LOGIT_ROWS, D), F32).at[0:N_GROUPS].set(wg.T).at[8:8 + N_EXPERTS].set(we.T)
    bt = jnp.zeros((LOGIT_ROWS, 1), F32).at[0:N_GROUPS, 0].set(bg).at[8:8 + N_EXPERTS, 0].set(be)
    wt_hi = wt.astype(BF16)
    wt_parts = jnp.concatenate([wt_hi, (wt - wt_hi.astype(F32)).astype(BF16)], axis=0)
    return pl.pallas_call(
        _router_kernel,
        out_shape=(
            jax.ShapeDtypeStruct((2, N), jnp.int32),
            jax.ShapeDtypeStruct((2, N), F32),
            jax.ShapeDtypeStruct((N_EXPERTS, 128), F32),
        ),
        grid=(N // TM,),
        in_specs=[
            pl.BlockSpec((TM, D), lambda i: (i, 0)),
            pl.BlockSpec((1, D), lambda i: (0, 0)),
            pl.BlockSpec((2 * LOGIT_ROWS, D), lambda i: (0, 0)),
            pl.BlockSpec((LOGIT_ROWS, 1), lambda i: (0, 0)),
        ],
        out_specs=(
            pl.BlockSpec((2, TM), lambda i: (0, i)),
            pl.BlockSpec((2, TM), lambda i: (0, i)),
            pl.BlockSpec((N_EXPERTS, 128), lambda i: (0, 0)),
        ),
        compiler_params=_cparams(("arbitrary",)),
        name="moe_router",
    )(x2d, g.reshape(1, D), wt_parts, bt)


def _rank_kernel(e_ref, start_ref, tri_ref, low_ref, lpos_ref, rstart_ref, rlen_ref, roff_ref, carry_ref):
    i = pl.program_id(0)

    @pl.when(i == 0)
    def _():
        carry_ref[...] = jnp.zeros(carry_ref.shape, F32)

    TM = e_ref.shape[1]
    io32 = lax.broadcasted_iota(jnp.int32, (N_EXPERTS, TM), 0)
    base = carry_ref[...] + start_ref[...]
    oh = [io32 == e_ref[k:k + 1, :] for k in range(2)]
    ohf = [o.astype(F32) for o in oh]
    before = [jnp.dot(o.astype(BF16), tri_ref[...], preferred_element_type=F32) for o in ohf]
    cnt = [jnp.sum(o, axis=1, keepdims=True) for o in ohf]
    run_len = cnt[0] + cnt[1]
    lanes = rlen_ref.shape[2]
    run_off = jnp.dot(low_ref[...], jnp.broadcast_to(run_len, (N_EXPERTS, lanes)),
                      precision=lax.Precision.HIGHEST, preferred_element_type=F32)
    for k in range(2):
        in_run = before[k] + (cnt[0] if k == 1 else 0.0)
        lpos = jnp.sum(jnp.where(oh[k], in_run + run_off[:, 0:1], 0.0), axis=0, keepdims=True)
        lpos_ref[k:k + 1, :] = lpos.astype(jnp.int32)
    rstart_ref[0] = jnp.broadcast_to(base, (N_EXPERTS, lanes)).astype(jnp.int32)
    rlen_ref[0] = jnp.broadcast_to(run_len, (N_EXPERTS, lanes)).astype(jnp.int32)
    roff_ref[0] = run_off.astype(jnp.int32)
    carry_ref[...] = carry_ref[...] + run_len


def _rank(e_t, seg_start):
    _, N = e_t.shape
    TM = ROW_TILE
    n_tiles = N // TM
    tri = (jnp.arange(TM)[:, None] < jnp.arange(TM)[None, :]).astype(BF16)
    low = (jnp.arange(N_EXPERTS)[None, :] < jnp.arange(N_EXPERTS)[:, None]).astype(F32)
    tok = jax.ShapeDtypeStruct((2, N), jnp.int32)
    run = jax.ShapeDtypeStruct((n_tiles, N_EXPERTS, LANES), jnp.int32)
    tok_spec = pl.BlockSpec((2, TM), lambda i: (0, i))
    run_spec = pl.BlockSpec((1, N_EXPERTS, LANES), lambda i: (i, 0, 0))
    lpos_t, rstart, rlen, roff = pl.pallas_call(
        _rank_kernel,
        out_shape=(tok, run, run, run),
        grid=(n_tiles,),
        in_specs=[
            pl.BlockSpec((2, TM), lambda i: (0, i)),
            pl.BlockSpec((N_EXPERTS, 1), lambda i: (0, 0)),
            pl.BlockSpec((TM, TM), lambda i: (0, 0)),
            pl.BlockSpec((N_EXPERTS, N_EXPERTS), lambda i: (0, 0)),
        ],
        out_specs=(tok_spec, run_spec, run_spec, run_spec),
        scratch_shapes=[pltpu.VMEM((N_EXPERTS, 1), F32)],
        compiler_params=_cparams(("arbitrary",)),
        name="moe_rank",
    )(e_t, seg_start.astype(F32).reshape(N_EXPERTS, 1), tri, low)
    runs = tuple(r[:, :, 0].reshape(-1) for r in (rstart, rlen, roff))
    return lpos_t, runs


def _store_token_tiles(dst_ref, val):
    rows, d = val.shape
    nch = d // LANES
    for c in range(nch):
        dst_ref[pl.ds(c, rows, stride=nch), :] = val[:, c * LANES:(c + 1) * LANES]


def _load_token_tiles(src_ref, rows):
    nch = src_ref.shape[0] // rows
    return jnp.concatenate([src_ref[pl.ds(c, rows, stride=nch), :] for c in range(nch)], axis=1)


def _for_each_run_piece(rstart_ref, rlen_ref, roff_ref, tile, max_len, emit):
    for e in range(N_EXPERTS):
        n = rlen_ref[tile * N_EXPERTS + e]
        first_pos = roff_ref[tile * N_EXPERTS + e]
        first_slot = rstart_ref[tile * N_EXPERTS + e]
        size = max_len
        while size >= 1:
            @pl.when((n & size) != 0)
            def _(first_pos=first_pos, first_slot=first_slot, size=size):
                emit(first_pos, first_slot, size)
            first_pos = first_pos + (n & size)
            first_slot = first_slot + (n & size)
            size //= 2


def _dispatch_kernel(rstart_ref, rlen_ref, roff_ref, fill_ref, npad_ref, nu_ref,
                     lpos_ref, x_ref, g_ref, xs_ref, hs_ref, sem, zsem):
    TM = ROW_TILE
    D = x_ref.shape[1]
    nch = D // LANES
    BLK = EXPERT_BLOCK
    n_blocks = xs_ref.shape[0] // (BLK * nch)
    i = pl.program_id(0)

    def wait_copies(buf):
        for _ in range(2):
            pltpu.make_async_copy(hs_ref.at[buf, pl.ds(0, TM * nch)], xs_ref.at[pl.ds(0, TM * nch)],
                                  sem.at[buf]).wait()

    for buf in range(DISPATCH_TILES):
        rows = slice(buf * TM, (buf + 1) * TM)
        tile = i * DISPATCH_TILES + buf
        h = _rms(x_ref[rows, :], g_ref[...], RMS_EPS).astype(BF16)
        pos = lax.broadcasted_iota(jnp.int32, (2 * TM, TM), 0)
        perm = jnp.where(pos == lpos_ref[0:1, rows], 1.0, jnp.where(pos == lpos_ref[1:2, rows], 1.0, 0.0))
        sorted_rows = jnp.dot(perm.astype(BF16), h, preferred_element_type=F32)

        @pl.when(i >= 1)
        def _(buf=buf):
            wait_copies(buf)

        _store_token_tiles(hs_ref.at[buf], sorted_rows)

        def put_piece(first_pos, first_slot, size, buf=buf):
            pltpu.make_async_copy(
                hs_ref.at[buf, pl.ds(pl.multiple_of(first_pos * nch, nch), size * nch)],
                xs_ref.at[pl.ds(pl.multiple_of(first_slot * nch, nch), size * nch)], sem.at[buf]).start()

        _for_each_run_piece(rstart_ref, rlen_ref, roff_ref, tile, TM, put_piece)

    @pl.when(i == pl.num_programs(0) - 1)
    def _():
        for buf in range(DISPATCH_TILES):
            wait_copies(buf)
        hs_ref[0] = jnp.zeros(hs_ref.shape[1:], hs_ref.dtype)

        def zero_copy(first_slot, n_slots):
            return pltpu.make_async_copy(hs_ref.at[0, pl.ds(0, n_slots * nch)],
                                         xs_ref.at[pl.ds(pl.multiple_of(first_slot * nch, nch), n_slots * nch)], zsem)

        def pad_copies(act):
            for e in range(N_EXPERTS):
                first = fill_ref[e]
                n = npad_ref[e]
                size = BLK // 2
                while size >= 1:
                    @pl.when((n & size) != 0)
                    def _(first=first, size=size):
                        act(zero_copy(first, size))
                    first = first + (n & size)
                    size //= 2
            for b in range(N_EXPERTS):
                @pl.when(nu_ref[0] + b < n_blocks)
                def _(b=b):
                    act(zero_copy((nu_ref[0] + b) * BLK, BLK))

        pad_copies(lambda c: c.start())
        pad_copies(lambda c: c.wait())


def _dispatch(x2d, g, lpos_t, runs, n_slots, fill_start, n_pad, n_used):
    N, D = x2d.shape
    TM = ROW_TILE
    nch = D // LANES
    assert EXPERT_BLOCK <= 2 * TM
    T = DISPATCH_TILES
    return pl.pallas_call(
        _dispatch_kernel,
        out_shape=jax.ShapeDtypeStruct((n_slots * nch, LANES), F32),
        grid_spec=pltpu.PrefetchScalarGridSpec(
            num_scalar_prefetch=6,
            grid=(N // (T * TM),),
            in_specs=[
                pl.BlockSpec((2, T * TM), lambda i, *_: (0, i)),
                pl.BlockSpec((T * TM, D), lambda i, *_: (i, 0)),
                pl.BlockSpec((1, D), lambda i, *_: (0, 0)),
            ],
            out_specs=pl.BlockSpec(memory_space=pl.ANY),
            scratch_shapes=[pltpu.VMEM((T, 2 * TM * nch, LANES), F32), pltpu.SemaphoreType.DMA((T,)),
                            pltpu.SemaphoreType.DMA],
        ),
        compiler_params=_cparams(("arbitrary",)),
        name="moe_dispatch",
    )(*runs, fill_start, n_pad, n_used, lpos_t, x2d, g.reshape(1, D))


def _expert_kernel(be_ref, nu_ref, xs_ref, w1_ref, w3_ref, w2_ref, ys_ref, w13b, w2b):
    i = pl.program_id(0)
    BLK = EXPERT_BLOCK
    DE = w2b.shape[0]

    @pl.when(i < nu_ref[0])
    def _():
        prev = be_ref[jnp.maximum(i - 1, 0)]

        @pl.when((i == 0) | (be_ref[i] != prev))
        def _():
            w13b[:, 0:DE] = w1_ref[0].astype(BF16)
            w13b[:, DE:2 * DE] = w3_ref[0].astype(BF16)
            w2b[...] = w2_ref[0].astype(BF16)

        xb = _load_token_tiles(xs_ref, BLK).astype(BF16)
        ab = jnp.dot(xb, w13b[...], preferred_element_type=F32)
        a = ab[:, 0:DE]
        b = ab[:, DE:2 * DE]
        act = a * (1.0 / (1.0 + jnp.exp(-a))) * b
        _store_token_tiles(ys_ref, jnp.dot(act.astype(BF16), w2b[...], preferred_element_type=F32))

    @pl.when(i >= nu_ref[0])
    def _():
        ys_ref[...] = jnp.zeros(ys_ref.shape, F32)


def _experts(xs, block_expert, n_used, w1, w3, w2, layer):
    D, DE = w1.shape[2], w1.shape[3]
    nch = D // LANES
    BLK = EXPERT_BLOCK
    n_blocks = xs.shape[0] // (BLK * nch)

    def xmap(i, be, nu):
        return (jnp.minimum(i, jnp.maximum(nu[0] - 1, 0)), 0)

    def wmap(i, be, nu):
        return (layer, be[i], 0, 0)

    return pl.pallas_call(
        _expert_kernel,
        out_shape=jax.ShapeDtypeStruct(xs.shape, F32),
        grid_spec=pltpu.PrefetchScalarGridSpec(
            num_scalar_prefetch=2,
            grid=(n_blocks,),
            in_specs=[
                pl.BlockSpec((BLK * nch, LANES), xmap),
                pl.BlockSpec((None, 1, D, DE), wmap),
                pl.BlockSpec((None, 1, D, DE), wmap),
                pl.BlockSpec((None, 1, DE, D), wmap),
            ],
            out_specs=pl.BlockSpec((BLK * nch, LANES), lambda i, be, nu: (i, 0)),
            scratch_shapes=[
                pltpu.VMEM((D, 2 * DE), BF16),
                pltpu.VMEM((DE, D), BF16),
            ],
        ),
        compiler_params=_cparams(("arbitrary",)),
        name="moe_experts",
    )(block_expert, n_used, xs, w1, w3, w2)


def _combine_kernel(rstart_ref, rlen_ref, roff_ref, lpos_ref, x_ref, gate_ref, ys_ref, g_ref, o_ref,
                    sbuf, ybuf, sem, *, final_norm):
    TM = ROW_TILE
    D = x_ref.shape[1]
    nch = D // LANES
    i = pl.program_id(0)

    def gather_runs(tile, buf):
        def get_piece(first_pos, first_slot, size):
            pltpu.make_async_copy(
                ys_ref.at[pl.ds(pl.multiple_of(first_slot * nch, nch), size * nch)],
                sbuf.at[buf, pl.ds(pl.multiple_of(first_pos * nch, nch), size * nch)], sem.at[buf]).start()

        _for_each_run_piece(rstart_ref, rlen_ref, roff_ref, tile, TM, get_piece)

    @pl.when(i == 0)
    def _():
        for buf in range(COMBINE_TILES):
            gather_runs(buf, buf)

    for buf in range(COMBINE_TILES):
        tile = i * COMBINE_TILES + buf
        rows = slice(buf * TM, (buf + 1) * TM)
        for _ in range(2):
            pltpu.make_async_copy(ys_ref.at[pl.ds(0, TM * nch)], sbuf.at[buf, pl.ds(0, TM * nch)],
                                  sem.at[buf]).wait()

        def to_token_order(t, carry, buf=buf):
            tok = pl.ds(pl.multiple_of(t * nch, nch), nch)
            packed = lpos_ref[buf * TM + t]
            pos0 = packed & 0xFFFF
            pos1 = lax.shift_right_logical(packed, 16)
            ybuf[0, tok, :] = sbuf[buf, pl.ds(pl.multiple_of(pos0 * nch, nch), nch), :]
            ybuf[1, tok, :] = sbuf[buf, pl.ds(pl.multiple_of(pos1 * nch, nch), nch), :]
            return carry

        lax.fori_loop(0, TM, to_token_order, 0, unroll=8)

        @pl.when(i + 1 < pl.num_programs(0))
        def _(tile=tile, buf=buf):
            gather_runs(tile + COMBINE_TILES, buf)

        gate = gate_ref[rows, :]
        y = (x_ref[rows, :] + gate[:, 0:1] * _load_token_tiles(ybuf.at[0], TM)
             + gate[:, 1:2] * _load_token_tiles(ybuf.at[1], TM))
        if final_norm:
            y = _rms(y, g_ref[...], RMS_EPS)
        o_ref[rows, :] = y


def _combine(x2d, lpos_t, gate_t, runs, ys, g, final_norm):
    N, D = x2d.shape
    TM = ROW_TILE
    T = COMBINE_TILES
    nch = D // LANES
    n_steps = N // (T * TM)
    assert 2 * TM <= 1 << 16
    packed_pos = lpos_t[0] | (lpos_t[1] << 16)
    gate = gate_t.T
    return pl.pallas_call(
        functools.partial(_combine_kernel, final_norm=final_norm),
        out_shape=jax.ShapeDtypeStruct((N, D), F32),
        grid_spec=pltpu.PrefetchScalarGridSpec(
            num_scalar_prefetch=3,
            grid=(n_steps,),
            in_specs=[
                pl.BlockSpec((T * TM,), lambda i, *_: (i,), memory_space=pltpu.SMEM),
                pl.BlockSpec((T * TM, D), lambda i, *_: (i, 0)),
                pl.BlockSpec((T * TM, 2), lambda i, *_: (i, 0)),
                pl.BlockSpec(memory_space=pl.ANY),
                pl.BlockSpec((1, D), lambda i, *_: (0, 0)),
            ],
            out_specs=pl.BlockSpec((T * TM, D), lambda i, *_: (i, 0)),
            scratch_shapes=[pltpu.VMEM((T, 2 * TM * nch, LANES), F32), pltpu.VMEM((2, TM * nch, LANES), F32),
                            pltpu.SemaphoreType.DMA((T,))],
        ),
        compiler_params=_cparams(("arbitrary",)),
        name="moe_combine",
    )(*runs, packed_pos, x2d, gate, ys, g.reshape(1, D))


def _moe_layer(x2d, norm_g, wg, bg, we, be, w1, w3, w2, layer, out_g, final_norm):
    N, D = x2d.shape
    BLK = EXPERT_BLOCK
    n_slots = 2 * N + N_EXPERTS * BLK
    e_t, gate_t, cnt = _router(x2d, norm_g, wg, bg, we, be)
    counts = cnt[:, 0].astype(jnp.int32)
    padded = ((counts + BLK - 1) // BLK) * BLK
    seg_end = jnp.cumsum(padded)
    seg_start = seg_end - padded
    n_blocks = n_slots // BLK
    block_row0 = jnp.arange(n_blocks, dtype=jnp.int32) * BLK
    block_expert = jnp.minimum(
        jnp.sum((seg_end[None, :] <= block_row0[:, None]).astype(jnp.int32), axis=1), N_EXPERTS - 1)
    n_used = (seg_end[-1:] // BLK).astype(jnp.int32)
    lpos_t, runs = _rank(e_t, seg_start)
    xs = _dispatch(x2d, norm_g, lpos_t, runs, n_slots, seg_start + counts, padded - counts, n_used)
    ys = _experts(xs, block_expert, n_used, w1, w3, w2, layer)
    return _combine(x2d, lpos_t, gate_t, runs, ys, out_g, final_norm)


def _qkv_kernel(x_ref, g_ref, wq_ref, wk_ref, wv_ref, q_ref, k_ref, v_ref):
    h = _rms(x_ref[...], g_ref[...], RMS_EPS).astype(BF16)
    q = jnp.dot(h, wq_ref[...], preferred_element_type=F32)
    q_ref[...] = (q * (LOG2E * HEAD_DIM ** -0.5)).astype(BF16)
    k_ref[...] = jnp.dot(h, wk_ref[...], preferred_element_type=F32).astype(BF16)
    v_ref[...] = jnp.dot(h, wv_ref[...], preferred_element_type=F32).astype(BF16)


def _qkv(x2d, g, wq, wk, wv):
    N, D = x2d.shape
    DA = wq.shape[1]
    TM = ROW_TILE
    wspec = pl.BlockSpec((D, DA), lambda i: (0, 0))
    ospec = pl.BlockSpec((TM, DA), lambda i: (i, 0))
    osh = jax.ShapeDtypeStruct((N, DA), BF16)
    return pl.pallas_call(
        _qkv_kernel,
        out_shape=(osh, osh, osh),
        grid=(N // TM,),
        in_specs=[pl.BlockSpec((TM, D), lambda i: (i, 0)), pl.BlockSpec((1, D), lambda i: (0, 0)),
                  wspec, wspec, wspec],
        out_specs=(ospec, ospec, ospec),
        compiler_params=_cparams(("arbitrary",)),
        name="attn_qkv",
    )(x2d, g.reshape(1, D), wq.astype(BF16), wk.astype(BF16), wv.astype(BF16))


def _attn_kernel(slope_ref, q_ref, k_ref, v_ref, db_ref, lam_ref, sg_ref, o_ref,
                 qz_ref, s_ref, p_ref, m_ref, l_ref, acc_ref, *, lam_init):
    TQ = q_ref.shape[0]
    TK = ATTN_TK
    hd = pl.program_id(1)
    qi = pl.program_id(2)
    slope = slope_ref[hd]

    q = q_ref[...]
    lane = lax.broadcasted_iota(jnp.int32, q.shape, 1)
    zero = jnp.zeros_like(q)
    qz_ref[0:TQ, :] = jnp.where(lane < HEAD_DIM, q, zero)
    qz_ref[TQ:2 * TQ, :] = jnp.where(lane >= HEAD_DIM, q, zero)
    m_ref[...] = jnp.full(m_ref.shape, -jnp.inf, F32)
    l_ref[...] = jnp.zeros(l_ref.shape, F32)
    acc_ref[...] = jnp.zeros(acc_ref.shape, F32)
    LW = acc_ref.shape[1]
    NC = TK // LW
    kpos = lax.broadcasted_iota(jnp.int32, (1, TK), 1)
    q0 = qi * TQ

    def tile(j, diagonal):
        koff = pl.multiple_of(j * TK, TK)
        s = lax.dot_general(qz_ref[...], k_ref[pl.ds(koff, TK), :], (((1,), (1,)), ((), ())),
                            preferred_element_type=F32)
        if diagonal:
            db = db_ref[qi % (TK // TQ)]
            s = jnp.concatenate([s[0:TQ] + db, s[TQ:2 * TQ] + db], axis=0)
        else:
            s = s + slope * (kpos + (j * TK - q0)).astype(F32)
        s_ref[...] = s
        mx = s[:, 0:LW]
        for c in range(1, NC):
            mx = jnp.maximum(mx, s[:, c * LW:(c + 1) * LW])
        m_old = m_ref[...]
        m_new = jnp.maximum(m_old, jnp.max(mx, axis=-1, keepdims=True))
        alpha = jnp.exp2(m_old - m_new)
        m_ref[...] = m_new
        psum = None
        for c in range(NC):
            p = jnp.exp2(s_ref[:, c * LW:(c + 1) * LW] - m_new)
            p_ref[:, c * LW:(c + 1) * LW] = p.astype(BF16)
            psum = p if psum is None else psum + p
        l_ref[...] = alpha * l_ref[...] + psum
        acc_ref[...] = alpha * acc_ref[...] + jnp.dot(p_ref[...], v_ref[pl.ds(koff, TK), :],
                                                      preferred_element_type=F32)

    def past_tile(j, carry):
        tile(j, False)
        return carry

    n_past = q0 // TK
    lax.fori_loop(0, n_past, past_tile, 0)
    tile(n_past, True)

    lamv = lam_ref[...]
    lam = (jnp.exp(jnp.sum(lamv[0:1] * lamv[1:2], axis=-1, keepdims=True))
           - jnp.exp(jnp.sum(lamv[2:3] * lamv[3:4], axis=-1, keepdims=True)) + lam_init)
    on = acc_ref[...] / jnp.sum(l_ref[...], axis=-1, keepdims=True)
    o = on[0:TQ] - lam * on[TQ:2 * TQ]
    o = _rms(o, sg_ref[...], SUBLN_EPS) * (1.0 - lam_init)
    o_ref[...] = o.astype(o_ref.dtype)


def _attention(q, k, v, lamv, subln_g, lam_init):
    B, S, DA = q.shape
    HW = 2 * HEAD_DIM
    TQ = ATTN_TQ
    TK = ATTN_TK
    assert TK % TQ == 0 and S % TK == 0 and TQ % CHUNK == 0
    n_par = TK // TQ
    slopes = jnp.array([LOG2E * 2.0 ** (-8.0 * (h + 1) / N_HEADS) for h in range(N_HEADS)], F32)
    qr = jnp.arange(TQ, dtype=jnp.int32)[None, :, None]
    kc = jnp.arange(TK, dtype=jnp.int32)[None, None, :] - jnp.arange(n_par, dtype=jnp.int32)[:, None, None] * TQ
    rel = jnp.where(kc <= qr, kc, 2 * qr - kc).astype(F32)
    allowed = jnp.floor_divide(kc, CHUNK) <= qr // CHUNK
    diag_bias = jnp.where(allowed[None], slopes[:, None, None, None] * rel[None], NEG_INF)
    return pl.pallas_call(
        functools.partial(_attn_kernel, lam_init=lam_init),
        out_shape=jax.ShapeDtypeStruct((B, S, DA), BF16),
        grid_spec=pltpu.PrefetchScalarGridSpec(
            num_scalar_prefetch=1,
            grid=(B, N_HEADS, S // TQ),
            in_specs=[
                pl.BlockSpec((None, TQ, HW), lambda b, h, i, sl: (b, i, h)),
                pl.BlockSpec((None, S, HW), lambda b, h, i, sl: (b, 0, h)),
                pl.BlockSpec((None, S, HW), lambda b, h, i, sl: (b, 0, h)),
                pl.BlockSpec((None, n_par, TQ, TK), lambda b, h, i, sl: (h, 0, 0, 0)),
                pl.BlockSpec((4, HEAD_DIM), lambda b, h, i, sl: (0, 0)),
                pl.BlockSpec((1, HW), lambda b, h, i, sl: (0, 0)),
            ],
            out_specs=pl.BlockSpec((None, TQ, HW), lambda b, h, i, sl: (b, i, h)),
            scratch_shapes=[
                pltpu.VMEM((2 * TQ, HW), BF16),
                pltpu.VMEM((2 * TQ, TK), F32),
                pltpu.VMEM((2 * TQ, TK), BF16),
                pltpu.VMEM((2 * TQ, HW), F32),
                pltpu.VMEM((2 * TQ, HW), F32),
                pltpu.VMEM((2 * TQ, HW), F32),
            ],
        ),
        compiler_params=_cparams(("arbitrary", "arbitrary", "arbitrary")),
        name="diff_attention",
    )(slopes, q, k, v, diag_bias, lamv, subln_g.reshape(1, HW))


def _wo_kernel(o_ref, w_ref, x_ref, out_ref):
    out_ref[...] = x_ref[...] + jnp.dot(o_ref[...], w_ref[...], preferred_element_type=F32)


def _wo(o2d, wo, x2d):
    N, D = x2d.shape
    DA = o2d.shape[1]
    TM = ROW_TILE
    return pl.pallas_call(
        _wo_kernel,
        out_shape=jax.ShapeDtypeStruct((N, D), F32),
        grid=(N // TM,),
        in_specs=[pl.BlockSpec((TM, DA), lambda i: (i, 0)), pl.BlockSpec((DA, D), lambda i: (0, 0)),
                  pl.BlockSpec((TM, D), lambda i: (i, 0))],
        out_specs=pl.BlockSpec((TM, D), lambda i: (i, 0)),
        compiler_params=_cparams(("arbitrary",)),
        name="attn_wo",
    )(o2d, wo.astype(BF16), x2d)


def _attn_layer(x, g, wq, wk, wv, wo, lq1, lk1, lq2, lk2, subln_g, layer_idx):
    B, S, D = x.shape
    x2d = x.reshape(B * S, D)
    q, k, v = _qkv(x2d, g, wq, wk, wv)
    DA = q.shape[1]
    lam_init = 0.8 - 0.6 * math.exp(-0.3 * layer_idx)
    lamv = jnp.stack([lq1, lk1, lq2, lk2]).astype(F32)
    o = _attention(q.reshape(B, S, DA), k.reshape(B, S, DA), v.reshape(B, S, DA), lamv, subln_g, lam_init)
    return _wo(o.reshape(B * S, DA), wo, x2d).reshape(B, S, D)


def kernel(x, norm_mix_g, norm_ffn_g, pool_w, pool_scale, attn_wq, attn_wk, attn_wv, attn_wo,
           attn_lq1, attn_lk1, attn_lq2, attn_lk2, attn_subln_g, moe_wg, moe_bg, moe_we, moe_be,
           moe_w1, moe_w3, moe_w2, final_g):
    B, S, D = x.shape
    depth = norm_mix_g.shape[0]
    for i in range(depth):
        j = i // 2
        if i % 2 == 0:
            x = _pool_layer(x, norm_mix_g[i], pool_w[j], pool_scale[j])
        else:
            x = _attn_layer(x, norm_mix_g[i], attn_wq[j], attn_wk[j], attn_wv[j], attn_wo[j],
                            attn_lq1[j], attn_lk1[j], attn_lq2[j], attn_lk2[j], attn_subln_g[j], i)
        last = i == depth - 1
        x = _moe_layer(x.reshape(B * S, D), norm_ffn_g[i], moe_wg[i], moe_bg[i], moe_we[i], moe_be[i],
                       moe_w1, moe_w3, moe_w2, i, final_g, last).reshape(B, S, D)
    return x
```

```python
import functools
import math

import jax
import jax.numpy as jnp
from jax import lax
from jax.experimental import pallas as pl
from jax.experimental.pallas import tpu as pltpu

F32 = jnp.float32
BF16 = jnp.bfloat16

RMS_EPS = 1e-6
SUBLN_EPS = 1e-5
POOL_WINDOWS = (2, 4, 8, 16)
CHUNK = 64
N_HEADS = 8
HEAD_DIM = 64
N_GROUPS = 4
EXPERTS_PER_GROUP = 8
N_EXPERTS = N_GROUPS * EXPERTS_PER_GROUP
NEG_INF = -1e30
LOG2E = math.log2(math.e)

VMEM_LIMIT_BYTES = 48 * 1024 * 1024
POOL_TILE = 256
POOL_SUB = 4
ROW_TILE = 512
EXPERT_BLOCK = 512
DISPATCH_TILES = 2
COMBINE_TILES = 2
ATTN_TQ = 512
ATTN_TK = 512
LANES = 128
LOGIT_ROWS = 48


def _cparams(sem):
    return pltpu.CompilerParams(dimension_semantics=sem, vmem_limit_bytes=VMEM_LIMIT_BYTES)


def _rms(x, g, eps):
    return x * lax.rsqrt(jnp.mean(x * x, axis=-1, keepdims=True) + eps) * g


def _pool_kernel(x_ref, g_ref, band_ref, icnt_ref, w_ref, sc_ref, o_ref, hh_ref):
    T = POOL_TILE
    U = POOL_SUB
    si = pl.program_id(1)

    @pl.when(si == 0)
    def _():
        hh_ref[0:T, :] = jnp.zeros((T, hh_ref.shape[1]), BF16)

    @pl.when(si > 0)
    def _():
        hh_ref[0:T, :] = hh_ref[U * T:(U + 1) * T, :]

    gw = w_ref.shape[1]
    hs = []
    for u in range(U):
        h = _rms(x_ref[u * T:(u + 1) * T, :], g_ref[...], RMS_EPS)
        hh_ref[(u + 1) * T:(u + 2) * T, :] = h.astype(BF16)
        hs.append(h)
    for u in range(U):
        rows = slice(u * T, (u + 1) * T)
        outs = []
        for gi in range(len(POOL_WINDOWS)):
            cols = slice(gi * gw, (gi + 1) * gw)
            wsum = jnp.dot(band_ref[gi], hh_ref[u * T:(u + 2) * T, cols], preferred_element_type=F32)
            mix = wsum * icnt_ref[gi, rows, :] - hs[u][:, cols]
            outs.append(jnp.dot(mix.astype(BF16), w_ref[gi], preferred_element_type=F32))
        o_ref[rows, :] = x_ref[rows, :] + jnp.concatenate(outs, axis=-1) * sc_ref[...]


def _pool_layer(x, g, w, scale):
    B, S, D = x.shape
    T = POOL_TILE
    G = len(POOL_WINDOWS)
    r = jnp.arange(T)[:, None]
    c = jnp.arange(2 * T)[None, :]
    band = jnp.stack([((c <= r + T) & (c > r + T - win)) for win in POOL_WINDOWS]).astype(BF16)
    pos = jnp.arange(S, dtype=jnp.int32)
    icnt = jnp.stack([1.0 / jnp.minimum(pos + 1, win).astype(F32) for win in POOL_WINDOWS])
    icnt = icnt.reshape(G, S, 1)
    U = POOL_SUB
    return pl.pallas_call(
        _pool_kernel,
        out_shape=jax.ShapeDtypeStruct((B, S, D), F32),
        grid=(B, S // (U * T)),
        in_specs=[
            pl.BlockSpec((None, U * T, D), lambda b, s: (b, s, 0)),
            pl.BlockSpec((1, D), lambda b, s: (0, 0)),
            pl.BlockSpec((G, T, 2 * T), lambda b, s: (0, 0, 0)),
            pl.BlockSpec((G, U * T, 1), lambda b, s: (0, s, 0)),
            pl.BlockSpec((G, D // G, D // G), lambda b, s: (0, 0, 0)),
            pl.BlockSpec((1, D), lambda b, s: (0, 0)),
        ],
        out_specs=pl.BlockSpec((None, U * T, D), lambda b, s: (b, s, 0)),
        scratch_shapes=[pltpu.VMEM(((U + 1) * T, D), BF16)],
        compiler_params=_cparams(("arbitrary", "arbitrary")),
        name="pool_mixer",
    )(x, g.reshape(1, D), band, icnt, w.astype(BF16), scale.reshape(1, D))


def _router_kernel(x_ref, g_ref, wt_ref, b_ref, e_ref, gate_ref, cnt_ref):
    i = pl.program_id(0)
    x = x_ref[...]
    h = _rms(x, g_ref[...], RMS_EPS)
    h_hi = h.astype(BF16)
    h_lo = (h - h_hi.astype(F32)).astype(BF16)
    nt = (((1,), (1,)), ((), ()))
    both = lax.dot_general(wt_ref[...], h_hi, nt, preferred_element_type=F32)
    cross = lax.dot_general(wt_ref[0:LOGIT_ROWS, :], h_lo, nt, preferred_element_type=F32)
    lt = both[0:LOGIT_ROWS] + both[LOGIT_ROWS:2 * LOGIT_ROWS] + cross + b_ref[...]
    g0, g1, g2, g3 = lt[0:1], lt[1:2], lt[2:3], lt[3:4]
    gmax = jnp.maximum(jnp.maximum(g0, g1), jnp.maximum(g2, g3))
    gidx = jnp.where(g0 == gmax, 0, jnp.where(g1 == gmax, 1, jnp.where(g2 == gmax, 2, 3)))
    gsum = jnp.exp(g0 - gmax) + jnp.exp(g1 - gmax) + jnp.exp(g2 - gmax) + jnp.exp(g3 - gmax)
    g_p = 1.0 / gsum
    E = EXPERTS_PER_GROUP
    e_in = jnp.where(gidx == 0, lt[8:8 + E],
                     jnp.where(gidx == 1, lt[8 + E:8 + 2 * E],
                               jnp.where(gidx == 2, lt[8 + 2 * E:8 + 3 * E], lt[8 + 3 * E:8 + 4 * E])))
    io = lax.broadcasted_iota(jnp.int32, e_in.shape, 0)
    v0 = jnp.max(e_in, axis=0, keepdims=True)
    i0 = jnp.min(jnp.where(e_in == v0, io, E), axis=0, keepdims=True)
    rest = jnp.where(io == i0, -jnp.inf, e_in)
    v1 = jnp.max(rest, axis=0, keepdims=True)
    i1 = jnp.min(jnp.where(rest == v1, io, E), axis=0, keepdims=True)
    ex = jnp.exp(v1 - v0)
    p0 = 1.0 / (1.0 + ex)
    e0 = gidx * E + i0
    e1 = gidx * E + i1
    e_ref[0:1, :] = e0
    e_ref[1:2, :] = e1
    gate_ref[0:1, :] = g_p * p0
    gate_ref[1:2, :] = g_p * (ex * p0)
    io32 = lax.broadcasted_iota(jnp.int32, (N_EXPERTS, x.shape[0]), 0)
    oh = (io32 == e0).astype(F32) + (io32 == e1).astype(F32)
    tile_cnt = jnp.sum(oh, axis=1, keepdims=True)

    @pl.when(i == 0)
    def _():
        cnt_ref[...] = jnp.zeros(cnt_ref.shape, F32)

    cnt_ref[...] += jnp.broadcast_to(tile_cnt, cnt_ref.shape)


def _router(x2d, g, wg, bg, we, be):
    N, D = x2d.shape
    TM = ROW_TILE
    wt = jnp.zeros((LOGIT_ROWS, D), F32).at[0:N_GROUPS].set(wg.T).at[8:8 + N_EXPERTS].set(we.T)
    bt = jnp.zeros((LOGIT_ROWS, 1), F32).at[0:N_GROUPS, 0].set(bg).at[8:8 + N_EXPERTS, 0].set(be)
    wt_hi = wt.astype(BF16)
    wt_parts = jnp.concatenate([wt_hi, (wt - wt_hi.astype(F32)).astype(BF16)], axis=0)
    return pl.pallas_call(
        _router_kernel,
        out_shape=(
            jax.ShapeDtypeStruct((2, N), jnp.int32),
            jax.ShapeDtypeStruct((2, N), F32),
            jax.ShapeDtypeStruct((N_EXPERTS, 128), F32),
        ),
        grid=(N // TM,),
        in_specs=[
            pl.BlockSpec((TM, D), lambda i: (i, 0)),
            pl.BlockSpec((1, D), lambda i: (0, 0)),
            pl.BlockSpec((2 * LOGIT_ROWS, D), lambda i: (0, 0)),
            pl.BlockSpec((LOGIT_ROWS, 1), lambda i: (0, 0)),
        ],
        out_specs=(
            pl.BlockSpec((2, TM), lambda i: (0, i)),
            pl.BlockSpec((2, TM), lambda i: (0, i)),
            pl.BlockSpec((N_EXPERTS, 128), lambda i: (0, 0)),
        ),
        compiler_params=_cparams(("arbitrary",)),
        name="moe_router",
    )(x2d, g.reshape(1, D), wt_parts, bt)


def _rank_kernel(e_ref, start_ref, tri_ref, low_ref, lpos_ref, rstart_ref, rlen_ref, roff_ref, carry_ref):
    i = pl.program_id(0)

    @pl.when(i == 0)
    def _():
        carry_ref[...] = jnp.zeros(carry_ref.shape, F32)

    TM = e_ref.shape[1]
    io32 = lax.broadcasted_iota(jnp.int32, (N_EXPERTS, TM), 0)
    base = carry_ref[...] + start_ref[...]
    oh = [io32 == e_ref[k:k + 1, :] for k in range(2)]
    ohf = [o.astype(F32) for o in oh]
    before = [jnp.dot(o.astype(BF16), tri_ref[...], preferred_element_type=F32) for o in ohf]
    cnt = [jnp.sum(o, axis=1, keepdims=True) for o in ohf]
    run_len = cnt[0] + cnt[1]
    lanes = rlen_ref.shape[2]
    run_off = jnp.dot(low_ref[...], jnp.broadcast_to(run_len, (N_EXPERTS, lanes)),
                      precision=lax.Precision.HIGHEST, preferred_element_type=F32)
    for k in range(2):
        in_run = before[k] + (cnt[0] if k == 1 else 0.0)
        lpos = jnp.sum(jnp.where(oh[k], in_run + run_off[:, 0:1], 0.0), axis=0, keepdims=True)
        lpos_ref[k:k + 1, :] = lpos.astype(jnp.int32)
    rstart_ref[0] = jnp.broadcast_to(base, (N_EXPERTS, lanes)).astype(jnp.int32)
    rlen_ref[0] = jnp.broadcast_to(run_len, (N_EXPERTS, lanes)).astype(jnp.int32)
    roff_ref[0] = run_off.astype(jnp.int32)
    carry_ref[...] = carry_ref[...] + run_len


def _rank(e_t, seg_start):
    _, N = e_t.shape
    TM = ROW_TILE
    n_tiles = N // TM
    tri = (jnp.arange(TM)[:, None] < jnp.arange(TM)[None, :]).astype(BF16)
    low = (jnp.arange(N_EXPERTS)[None, :] < jnp.arange(N_EXPERTS)[:, None]).astype(F32)
    tok = jax.ShapeDtypeStruct((2, N), jnp.int32)
    run = jax.ShapeDtypeStruct((n_tiles, N_EXPERTS, LANES), jnp.int32)
    tok_spec = pl.BlockSpec((2, TM), lambda i: (0, i))
    run_spec = pl.BlockSpec((1, N_EXPERTS, LANES), lambda i: (i, 0, 0))
    lpos_t, rstart, rlen, roff = pl.pallas_call(
        _rank_kernel,
        out_shape=(tok, run, run, run),
        grid=(n_tiles,),
        in_specs=[
            pl.BlockSpec((2, TM), lambda i: (0, i)),
            pl.BlockSpec((N_EXPERTS, 1), lambda i: (0, 0)),
            pl.BlockSpec((TM, TM), lambda i: (0, 0)),
            pl.BlockSpec((N_EXPERTS, N_EXPERTS), lambda i: (0, 0)),
        ],
        out_specs=(tok_spec, run_spec, run_spec, run_spec),
        scratch_shapes=[pltpu.VMEM((N_EXPERTS, 1), F32)],
        compiler_params=_cparams(("arbitrary",)),
        name="moe_rank",
    )(e_t, seg_start.astype(F32).reshape(N_EXPERTS, 1), tri, low)
    runs = tuple(r[:, :, 0].reshape(-1) for r in (rstart, rlen, roff))
    return lpos_t, runs


def _store_token_tiles(dst_ref, val):
    rows, d = val.shape
    nch = d // LANES
    for c in range(nch):
        dst_ref[pl.ds(c, rows, stride=nch), :] = val[:, c * LANES:(c + 1) * LANES]


def _load_token_tiles(src_ref, rows):
    nch = src_ref.shape[0] // rows
    return jnp.concatenate([src_ref[pl.ds(c, rows, stride=nch), :] for c in range(nch)], axis=1)


def _for_each_run_piece(rstart_ref, rlen_ref, roff_ref, tile, max_rows, row_align, emit):
    for e in range(N_EXPERTS):
        n = rlen_ref[tile * N_EXPERTS + e]
        first_pos = roff_ref[tile * N_EXPERTS + e]
        first_slot = rstart_ref[tile * N_EXPERTS + e]
        size = max_rows
        while size >= row_align:
            @pl.when((n & size) != 0)
            def _(first_pos=first_pos, first_slot=first_slot, size=size):
                emit(pl.multiple_of(first_pos, row_align), pl.multiple_of(first_slot, row_align), size)
            first_pos = first_pos + (n & size)
            first_slot = first_slot + (n & size)
            size //= 2


def _dispatch_kernel(rstart_ref, rlen_ref, roff_ref, fill_ref, npad_ref, nu_ref,
                     lpos_ref, x_ref, g_ref, xs_ref, hs_ref, sem, zsem):
    TM = ROW_TILE
    D = x_ref.shape[1]
    nch = D // LANES
    BLK = EXPERT_BLOCK
    n_blocks = xs_ref.shape[0] // (BLK * nch)
    i = pl.program_id(0)

    def wait_copies(buf):
        for _ in range(2):
            pltpu.make_async_copy(hs_ref.at[buf, pl.ds(0, TM * nch)], xs_ref.at[pl.ds(0, TM * nch)],
                                  sem.at[buf]).wait()

    for buf in range(DISPATCH_TILES):
        rows = slice(buf * TM, (buf + 1) * TM)
        tile = i * DISPATCH_TILES + buf
        h = _rms(x_ref[rows, :], g_ref[...], RMS_EPS).astype(BF16)
        pos = lax.broadcasted_iota(jnp.int32, (2 * TM, TM), 0)
        perm = jnp.where(pos == lpos_ref[0:1, rows], 1.0, jnp.where(pos == lpos_ref[1:2, rows], 1.0, 0.0))
        sorted_rows = jnp.dot(perm.astype(BF16), h, preferred_element_type=F32)

        @pl.when(i >= 1)
        def _(buf=buf):
            wait_copies(buf)

        _store_token_tiles(hs_ref.at[buf], sorted_rows)

        def put_piece(pos_row, slot_row, n_rows, buf=buf):
            pltpu.make_async_copy(hs_ref.at[buf, pl.ds(pos_row, n_rows)], xs_ref.at[pl.ds(slot_row, n_rows)],
                                  sem.at[buf]).start()

        _for_each_run_piece(rstart_ref, rlen_ref, roff_ref, tile, TM * nch, nch, put_piece)

    @pl.when(i == pl.num_programs(0) - 1)
    def _():
        for buf in range(DISPATCH_TILES):
            wait_copies(buf)
        hs_ref[0] = jnp.zeros(hs_ref.shape[1:], hs_ref.dtype)

        def zero_copy(first_slot, n_slots):
            return pltpu.make_async_copy(hs_ref.at[0, pl.ds(0, n_slots * nch)],
                                         xs_ref.at[pl.ds(pl.multiple_of(first_slot * nch, nch), n_slots * nch)], zsem)

        def pad_copies(act):
            for e in range(N_EXPERTS):
                first = fill_ref[e]
                n = npad_ref[e]
                size = BLK // 2
                while size >= 1:
                    @pl.when((n & size) != 0)
                    def _(first=first, size=size):
                        act(zero_copy(first, size))
                    first = first + (n & size)
                    size //= 2
            for b in range(N_EXPERTS):
                @pl.when(nu_ref[0] + b < n_blocks)
                def _(b=b):
                    act(zero_copy((nu_ref[0] + b) * BLK, BLK))

        pad_copies(lambda c: c.start())
        pad_copies(lambda c: c.wait())


def _dispatch(x2d, g, lpos_t, runs, n_slots, fill_start, n_pad, n_used):
    N, D = x2d.shape
    TM = ROW_TILE
    nch = D // LANES
    assert EXPERT_BLOCK <= 2 * TM
    T = DISPATCH_TILES
    return pl.pallas_call(
        _dispatch_kernel,
        out_shape=jax.ShapeDtypeStruct((n_slots * nch, LANES), F32),
        grid_spec=pltpu.PrefetchScalarGridSpec(
            num_scalar_prefetch=6,
            grid=(N // (T * TM),),
            in_specs=[
                pl.BlockSpec((2, T * TM), lambda i, *_: (0, i)),
                pl.BlockSpec((T * TM, D), lambda i, *_: (i, 0)),
                pl.BlockSpec((1, D), lambda i, *_: (0, 0)),
            ],
            out_specs=pl.BlockSpec(memory_space=pl.ANY),
            scratch_shapes=[pltpu.VMEM((T, 2 * TM * nch, LANES), F32), pltpu.SemaphoreType.DMA((T,)),
                            pltpu.SemaphoreType.DMA],
        ),
        compiler_params=_cparams(("arbitrary",)),
        name="moe_dispatch",
    )(*runs, fill_start, n_pad, n_used, lpos_t, x2d, g.reshape(1, D))


def _expert_kernel(be_ref, nu_ref, xs_ref, w1_ref, w3_ref, w2_ref, ys_ref, w13b, w2b):
    i = pl.program_id(0)
    BLK = EXPERT_BLOCK
    DE = w2b.shape[0]

    @pl.when(i < nu_ref[0])
    def _():
        prev = be_ref[jnp.maximum(i - 1, 0)]

        @pl.when((i == 0) | (be_ref[i] != prev))
        def _():
            w13b[:, 0:DE] = w1_ref[0].astype(BF16)
            w13b[:, DE:2 * DE] = w3_ref[0].astype(BF16)
            w2b[...] = w2_ref[0].astype(BF16)

        xb = _load_token_tiles(xs_ref, BLK).astype(BF16)
        ab = jnp.dot(xb, w13b[...], preferred_element_type=F32)
        a = ab[:, 0:DE]
        b = ab[:, DE:2 * DE]
        act = a * (1.0 / (1.0 + jnp.exp(-a))) * b
        _store_token_tiles(ys_ref, jnp.dot(act.astype(BF16), w2b[...], preferred_element_type=F32))

    @pl.when(i >= nu_ref[0])
    def _():
        ys_ref[...] = jnp.zeros(ys_ref.shape, F32)


def _experts(xs, block_expert, n_used, w1, w3, w2, layer):
    D, DE = w1.shape[2], w1.shape[3]
    nch = D // LANES
    BLK = EXPERT_BLOCK
    n_blocks = xs.shape[0] // (BLK * nch)

    def xmap(i, be, nu):
        return (jnp.minimum(i, jnp.maximum(nu[0] - 1, 0)), 0)

    def wmap(i, be, nu):
        return (layer, be[i], 0, 0)

    return pl.pallas_call(
        _expert_kernel,
        out_shape=jax.ShapeDtypeStruct(xs.shape, F32),
        grid_spec=pltpu.PrefetchScalarGridSpec(
            num_scalar_prefetch=2,
            grid=(n_blocks,),
            in_specs=[
                pl.BlockSpec((BLK * nch, LANES), xmap),
                pl.BlockSpec((None, 1, D, DE), wmap),
                pl.BlockSpec((None, 1, D, DE), wmap),
                pl.BlockSpec((None, 1, DE, D), wmap),
            ],
            out_specs=pl.BlockSpec((BLK * nch, LANES), lambda i, be, nu: (i, 0)),
            scratch_shapes=[
                pltpu.VMEM((D, 2 * DE), BF16),
                pltpu.VMEM((DE, D), BF16),
            ],
        ),
        compiler_params=_cparams(("arbitrary",)),
        name="moe_experts",
    )(block_expert, n_used, xs, w1, w3, w2)


def _combine_kernel(rstart_ref, rlen_ref, roff_ref, lpos_ref, x_ref, gate_ref, ys_ref, g_ref, o_ref,
                    sbuf, ybuf, sem, *, final_norm):
    TM = ROW_TILE
    D = x_ref.shape[1]
    nch = D // LANES
    i = pl.program_id(0)

    def gather_runs(tile, buf):
        def get_piece(pos_row, slot_row, n_rows):
            pltpu.make_async_copy(ys_ref.at[pl.ds(slot_row, n_rows)], sbuf.at[buf, pl.ds(pos_row, n_rows)],
                                  sem.at[buf]).start()

        _for_each_run_piece(rstart_ref, rlen_ref, roff_ref, tile, TM * nch, nch, get_piece)

    @pl.when(i == 0)
    def _():
        for buf in range(COMBINE_TILES):
            gather_runs(buf, buf)

    for buf in range(COMBINE_TILES):
        tile = i * COMBINE_TILES + buf
        rows = slice(buf * TM, (buf + 1) * TM)
        for _ in range(2):
            pltpu.make_async_copy(ys_ref.at[pl.ds(0, TM * nch)], sbuf.at[buf, pl.ds(0, TM * nch)],
                                  sem.at[buf]).wait()

        def to_token_order(t, carry, buf=buf):
            tok = pl.ds(pl.multiple_of(t * nch, nch), nch)
            packed = lpos_ref[buf * TM + t]
            pos0 = packed & 0xFFFF
            pos1 = lax.shift_right_logical(packed, 16)
            ybuf[0, tok, :] = sbuf[buf, pl.ds(pl.multiple_of(pos0 * nch, nch), nch), :]
            ybuf[1, tok, :] = sbuf[buf, pl.ds(pl.multiple_of(pos1 * nch, nch), nch), :]
            return carry

        lax.fori_loop(0, TM, to_token_order, 0, unroll=8)

        @pl.when(i + 1 < pl.num_programs(0))
        def _(tile=tile, buf=buf):
            gather_runs(tile + COMBINE_TILES, buf)

        gate = gate_ref[rows, :]
        y = (x_ref[rows, :] + gate[:, 0:1] * _load_token_tiles(ybuf.at[0], TM)
             + gate[:, 1:2] * _load_token_tiles(ybuf.at[1], TM))
        if final_norm:
            y = _rms(y, g_ref[...], RMS_EPS)
        o_ref[rows, :] = y


def _combine(x2d, lpos_t, gate_t, runs, ys, g, final_norm):
    N, D = x2d.shape
    TM = ROW_TILE
    T = COMBINE_TILES
    nch = D // LANES
    n_steps = N // (T * TM)
    assert 2 * TM <= 1 << 16
    packed_pos = lpos_t[0] | (lpos_t[1] << 16)
    gate = gate_t.T
    return pl.pallas_call(
        functools.partial(_combine_kernel, final_norm=final_norm),
        out_shape=jax.ShapeDtypeStruct((N, D), F32),
        grid_spec=pltpu.PrefetchScalarGridSpec(
            num_scalar_prefetch=3,
            grid=(n_steps,),
            in_specs=[
                pl.BlockSpec((T * TM,), lambda i, *_: (i,), memory_space=pltpu.SMEM),
                pl.BlockSpec((T * TM, D), lambda i, *_: (i, 0)),
                pl.BlockSpec((T * TM, 2), lambda i, *_: (i, 0)),
                pl.BlockSpec(memory_space=pl.ANY),
                pl.BlockSpec((1, D), lambda i, *_: (0, 0)),
            ],
            out_specs=pl.BlockSpec((T * TM, D), lambda i, *_: (i, 0)),
            scratch_shapes=[pltpu.VMEM((T, 2 * TM * nch, LANES), F32), pltpu.VMEM((2, TM * nch, LANES), F32),
                            pltpu.SemaphoreType.DMA((T,))],
        ),
        compiler_params=_cparams(("arbitrary",)),
        name="moe_combine",
    )(*runs, packed_pos, x2d, gate, ys, g.reshape(1, D))


def _moe_layer(x2d, norm_g, wg, bg, we, be, w1, w3, w2, layer, out_g, final_norm):
    N, D = x2d.shape
    BLK = EXPERT_BLOCK
    n_slots = 2 * N + N_EXPERTS * BLK
    e_t, gate_t, cnt = _router(x2d, norm_g, wg, bg, we, be)
    counts = cnt[:, 0].astype(jnp.int32)
    padded = ((counts + BLK - 1) // BLK) * BLK
    seg_end = jnp.cumsum(padded)
    seg_start = seg_end - padded
    n_blocks = n_slots // BLK
    block_row0 = jnp.arange(n_blocks, dtype=jnp.int32) * BLK
    block_expert = jnp.minimum(
        jnp.sum((seg_end[None, :] <= block_row0[:, None]).astype(jnp.int32), axis=1), N_EXPERTS - 1)
    n_used = (seg_end[-1:] // BLK).astype(jnp.int32)
    lpos_t, runs = _rank(e_t, seg_start)
    runs = tuple(r * (D // LANES) for r in runs)
    xs = _dispatch(x2d, norm_g, lpos_t, runs, n_slots, seg_start + counts, padded - counts, n_used)
    ys = _experts(xs, block_expert, n_used, w1, w3, w2, layer)
    return _combine(x2d, lpos_t, gate_t, runs, ys, out_g, final_norm)


def _qkv_kernel(x_ref, g_ref, wq_ref, wk_ref, wv_ref, q_ref, k_ref, v_ref):
    h = _rms(x_ref[...], g_ref[...], RMS_EPS).astype(BF16)
    q = jnp.dot(h, wq_ref[...], preferred_element_type=F32)
    q_ref[...] = (q * (LOG2E * HEAD_DIM ** -0.5)).astype(BF16)
    k_ref[...] = jnp.dot(h, wk_ref[...], preferred_element_type=F32).astype(BF16)
    v_ref[...] = jnp.dot(h, wv_ref[...], preferred_element_type=F32).astype(BF16)


def _qkv(x2d, g, wq, wk, wv):
    N, D = x2d.shape
    DA = wq.shape[1]
    TM = ROW_TILE
    wspec = pl.BlockSpec((D, DA), lambda i: (0, 0))
    ospec = pl.BlockSpec((TM, DA), lambda i: (i, 0))
    osh = jax.ShapeDtypeStruct((N, DA), BF16)
    return pl.pallas_call(
        _qkv_kernel,
        out_shape=(osh, osh, osh),
        grid=(N // TM,),
        in_specs=[pl.BlockSpec((TM, D), lambda i: (i, 0)), pl.BlockSpec((1, D), lambda i: (0, 0)),
                  wspec, wspec, wspec],
        out_specs=(ospec, ospec, ospec),
        compiler_params=_cparams(("arbitrary",)),
        name="attn_qkv",
    )(x2d, g.reshape(1, D), wq.astype(BF16), wk.astype(BF16), wv.astype(BF16))


def _attn_kernel(slope_ref, q_ref, k_ref, v_ref, db_ref, lam_ref, sg_ref, o_ref,
                 qz_ref, s_ref, p_ref, m_ref, l_ref, acc_ref, *, lam_init):
    TQ = q_ref.shape[0]
    TK = ATTN_TK
    hd = pl.program_id(1)
    qi = pl.program_id(2)
    slope = slope_ref[hd]

    q = q_ref[...]
    lane = lax.broadcasted_iota(jnp.int32, q.shape, 1)
    zero = jnp.zeros_like(q)
    qz_ref[0:TQ, :] = jnp.where(lane < HEAD_DIM, q, zero)
    qz_ref[TQ:2 * TQ, :] = jnp.where(lane >= HEAD_DIM, q, zero)
    m_ref[...] = jnp.full(m_ref.shape, -jnp.inf, F32)
    l_ref[...] = jnp.zeros(l_ref.shape, F32)
    acc_ref[...] = jnp.zeros(acc_ref.shape, F32)
    LW = acc_ref.shape[1]
    NC = TK // LW
    kpos = lax.broadcasted_iota(jnp.int32, (1, TK), 1)
    q0 = qi * TQ

    def tile(j, diagonal):
        koff = pl.multiple_of(j * TK, TK)
        s = lax.dot_general(qz_ref[...], k_ref[pl.ds(koff, TK), :], (((1,), (1,)), ((), ())),
                            preferred_element_type=F32)
        if diagonal:
            db = db_ref[qi % (TK // TQ)]
            s = jnp.concatenate([s[0:TQ] + db, s[TQ:2 * TQ] + db], axis=0)
        else:
            s = s + slope * (kpos + (j * TK - q0)).astype(F32)
        s_ref[...] = s
        mx = s[:, 0:LW]
        for c in range(1, NC):
            mx = jnp.maximum(mx, s[:, c * LW:(c + 1) * LW])
        m_old = m_ref[...]
        m_new = jnp.maximum(m_old, jnp.max(mx, axis=-1, keepdims=True))
        alpha = jnp.exp2(m_old - m_new)
        m_ref[...] = m_new
        psum = None
        for c in range(NC):
            p = jnp.exp2(s_ref[:, c * LW:(c + 1) * LW] - m_new)
            p_ref[:, c * LW:(c + 1) * LW] = p.astype(BF16)
            psum = p if psum is None else psum + p
        l_ref[...] = alpha * l_ref[...] + psum
        acc_ref[...] = alpha * acc_ref[...] + jnp.dot(p_ref[...], v_ref[pl.ds(koff, TK), :],
                                                      preferred_element_type=F32)

    def past_tile(j, carry):
        tile(j, False)
        return carry

    n_past = q0 // TK
    lax.fori_loop(0, n_past, past_tile, 0)
    tile(n_past, True)

    lamv = lam_ref[...]
    lam = (jnp.exp(jnp.sum(lamv[0:1] * lamv[1:2], axis=-1, keepdims=True))
           - jnp.exp(jnp.sum(lamv[2:3] * lamv[3:4], axis=-1, keepdims=True)) + lam_init)
    on = acc_ref[...] / jnp.sum(l_ref[...], axis=-1, keepdims=True)
    o = on[0:TQ] - lam * on[TQ:2 * TQ]
    o = _rms(o, sg_ref[...], SUBLN_EPS) * (1.0 - lam_init)
    o_ref[...] = o.astype(o_ref.dtype)


def _attention(q, k, v, lamv, subln_g, lam_init):
    B, S, DA = q.shape
    HW = 2 * HEAD_DIM
    TQ = ATTN_TQ
    TK = ATTN_TK
    assert TK % TQ == 0 and S % TK == 0 and TQ % CHUNK == 0
    n_par = TK // TQ
    slopes = jnp.array([LOG2E * 2.0 ** (-8.0 * (h + 1) / N_HEADS) for h in range(N_HEADS)], F32)
    qr = jnp.arange(TQ, dtype=jnp.int32)[None, :, None]
    kc = jnp.arange(TK, dtype=jnp.int32)[None, None, :] - jnp.arange(n_par, dtype=jnp.int32)[:, None, None] * TQ
    rel = jnp.where(kc <= qr, kc, 2 * qr - kc).astype(F32)
    allowed = jnp.floor_divide(kc, CHUNK) <= qr // CHUNK
    diag_bias = jnp.where(allowed[None], slopes[:, None, None, None] * rel[None], NEG_INF)
    return pl.pallas_call(
        functools.partial(_attn_kernel, lam_init=lam_init),
        out_shape=jax.ShapeDtypeStruct((B, S, DA), BF16),
        grid_spec=pltpu.PrefetchScalarGridSpec(
            num_scalar_prefetch=1,
            grid=(B, N_HEADS, S // TQ),
            in_specs=[
                pl.BlockSpec((None, TQ, HW), lambda b, h, i, sl: (b, i, h)),
                pl.BlockSpec((None, S, HW), lambda b, h, i, sl: (b, 0, h)),
                pl.BlockSpec((None, S, HW), lambda b, h, i, sl: (b, 0, h)),
                pl.BlockSpec((None, n_par, TQ, TK), lambda b, h, i, sl: (h, 0, 0, 0)),
                pl.BlockSpec((4, HEAD_DIM), lambda b, h, i, sl: (0, 0)),
                pl.BlockSpec((1, HW), lambda b, h, i, sl: (0, 0)),
            ],
            out_specs=pl.BlockSpec((None, TQ, HW), lambda b, h, i, sl: (b, i, h)),
            scratch_shapes=[
                pltpu.VMEM((2 * TQ, HW), BF16),
                pltpu.VMEM((2 * TQ, TK), F32),
                pltpu.VMEM((2 * TQ, TK), BF16),
                pltpu.VMEM((2 * TQ, HW), F32),
                pltpu.VMEM((2 * TQ, HW), F32),
                pltpu.VMEM((2 * TQ, HW), F32),
            ],
        ),
        compiler_params=_cparams(("arbitrary", "arbitrary", "arbitrary")),
        name="diff_attention",
    )(slopes, q, k, v, diag_bias, lamv, subln_g.reshape(1, HW))


def _wo_kernel(o_ref, w_ref, x_ref, out_ref):
    out_ref[...] = x_ref[...] + jnp.dot(o_ref[...], w_ref[...], preferred_element_type=F32)


def _wo(o2d, wo, x2d):
    N, D = x2d.shape
    DA = o2d.shape[1]
    TM = ROW_TILE
    return pl.pallas_call(
        _wo_kernel,
        out_shape=jax.ShapeDtypeStruct((N, D), F32),
        grid=(N // TM,),
        in_specs=[pl.BlockSpec((TM, DA), lambda i: (i, 0)), pl.BlockSpec((DA, D), lambda i: (0, 0)),
                  pl.BlockSpec((TM, D), lambda i: (i, 0))],
        out_specs=pl.BlockSpec((TM, D), lambda i: (i, 0)),
        compiler_params=_cparams(("arbitrary",)),
        name="attn_wo",
    )(o2d, wo.astype(BF16), x2d)


def _attn_layer(x, g, wq, wk, wv, wo, lq1, lk1, lq2, lk2, subln_g, layer_idx):
    B, S, D = x.shape
    x2d = x.reshape(B * S, D)
    q, k, v = _qkv(x2d, g, wq, wk, wv)
    DA = q.shape[1]
    lam_init = 0.8 - 0.6 * math.exp(-0.3 * layer_idx)
    lamv = jnp.stack([lq1, lk1, lq2, lk2]).astype(F32)
    o = _attention(q.reshape(B, S, DA), k.reshape(B, S, DA), v.reshape(B, S, DA), lamv, subln_g, lam_init)
    return _wo(o.reshape(B * S, DA), wo, x2d).reshape(B, S, D)


def kernel(x, norm_mix_g, norm_ffn_g, pool_w, pool_scale, attn_wq, attn_wk, attn_wv, attn_wo,
           attn_lq1, attn_lk1, attn_lq2, attn_lk2, attn_subln_g, moe_wg, moe_bg, moe_we, moe_be,
           moe_w1, moe_w3, moe_w2, final_g):
    B, S, D = x.shape
    depth = norm_mix_g.shape[0]
    for i in range(depth):
        j = i // 2
        if i % 2 == 0:
            x = _pool_layer(x, norm_mix_g[i], pool_w[j], pool_scale[j])
        else:
            x = _attn_layer(x, norm_mix_g[i], attn_wq[j], attn_wk[j], attn_wv[j], attn_wo[j],
                            attn_lq1[j], attn_lk1[j], attn_lq2[j], attn_lk2[j], attn_subln_g[j], i)
        last = i == depth - 1
        x = _moe_layer(x.reshape(B * S, D), norm_ffn_g[i], moe_wg[i], moe_bg[i], moe_we[i], moe_be[i],
                       moe_w1, moe_w3, moe_w2, i, final_g, last).reshape(B, S, D)
    return x
```

```python
import functools
import math

import jax
import jax.numpy as jnp
from jax import lax
from jax.experimental import pallas as pl
from jax.experimental.pallas import tpu as pltpu

F32 = jnp.float32
BF16 = jnp.bfloat16

RMS_EPS = 1e-6
SUBLN_EPS = 1e-5
POOL_WINDOWS = (2, 4, 8, 16)
CHUNK = 64
N_HEADS = 8
HEAD_DIM = 64
N_GROUPS = 4
EXPERTS_PER_GROUP = 8
N_EXPERTS = N_GROUPS * EXPERTS_PER_GROUP
NEG_INF = -1e30
LOG2E = math.log2(math.e)

VMEM_LIMIT_BYTES = 48 * 1024 * 1024
POOL_TILE = 256
POOL_SUB = 4
ROW_TILE = 512
EXPERT_BLOCK = 512
DISPATCH_TILES = 2
COMBINE_TILES = 2
ATTN_TQ = 512
ATTN_TK = 512
LANES = 128
LOGIT_ROWS = 48


def _cparams(sem):
    return pltpu.CompilerParams(dimension_semantics=sem, vmem_limit_bytes=VMEM_LIMIT_BYTES)


def _rms(x, g, eps):
    return x * lax.rsqrt(jnp.mean(x * x, axis=-1, keepdims=True) + eps) * g


def _pool_kernel(x_ref, g_ref, band_ref, icnt_ref, w_ref, sc_ref, o_ref, hh_ref):
    T = POOL_TILE
    U = POOL_SUB
    si = pl.program_id(1)

    @pl.when(si == 0)
    def _():
        hh_ref[0:T, :] = jnp.zeros((T, hh_ref.shape[1]), BF16)

    @pl.when(si > 0)
    def _():
        hh_ref[0:T, :] = hh_ref[U * T:(U + 1) * T, :]

    gw = w_ref.shape[1]
    hs = []
    for u in range(U):
        h = _rms(x_ref[u * T:(u + 1) * T, :], g_ref[...], RMS_EPS)
        hh_ref[(u + 1) * T:(u + 2) * T, :] = h.astype(BF16)
        hs.append(h)
    for u in range(U):
        rows = slice(u * T, (u + 1) * T)
        outs = []
        for gi in range(len(POOL_WINDOWS)):
            cols = slice(gi * gw, (gi + 1) * gw)
            wsum = jnp.dot(band_ref[gi], hh_ref[u * T:(u + 2) * T, cols], preferred_element_type=F32)
            mix = wsum * icnt_ref[gi, rows, :] - hs[u][:, cols]
            outs.append(jnp.dot(mix.astype(BF16), w_ref[gi], preferred_element_type=F32))
        o_ref[rows, :] = x_ref[rows, :] + jnp.concatenate(outs, axis=-1) * sc_ref[...]


def _pool_layer(x, g, w, scale):
    B, S, D = x.shape
    T = POOL_TILE
    G = len(POOL_WINDOWS)
    r = jnp.arange(T)[:, None]
    c = jnp.arange(2 * T)[None, :]
    band = jnp.stack([((c <= r + T) & (c > r + T - win)) for win in POOL_WINDOWS]).astype(BF16)
    pos = jnp.arange(S, dtype=jnp.int32)
    icnt = jnp.stack([1.0 / jnp.minimum(pos + 1, win).astype(F32) for win in POOL_WINDOWS])
    icnt = icnt.reshape(G, S, 1)
    U = POOL_SUB
    return pl.pallas_call(
        _pool_kernel,
        out_shape=jax.ShapeDtypeStruct((B, S, D), F32),
        grid=(B, S // (U * T)),
        in_specs=[
            pl.BlockSpec((None, U * T, D), lambda b, s: (b, s, 0)),
            pl.BlockSpec((1, D), lambda b, s: (0, 0)),
            pl.BlockSpec((G, T, 2 * T), lambda b, s: (0, 0, 0)),
            pl.BlockSpec((G, U * T, 1), lambda b, s: (0, s, 0)),
            pl.BlockSpec((G, D // G, D // G), lambda b, s: (0, 0, 0)),
            pl.BlockSpec((1, D), lambda b, s: (0, 0)),
        ],
        out_specs=pl.BlockSpec((None, U * T, D), lambda b, s: (b, s, 0)),
        scratch_shapes=[pltpu.VMEM(((U + 1) * T, D), BF16)],
        compiler_params=_cparams(("arbitrary", "arbitrary")),
        name="pool_mixer",
    )(x, g.reshape(1, D), band, icnt, w.astype(BF16), scale.reshape(1, D))


def _router_kernel(x_ref, g_ref, wt_ref, b_ref, e_ref, gate_ref, cnt_ref):
    _route(x_ref[...], g_ref, wt_ref, b_ref, e_ref, gate_ref, cnt_ref)


def _route(x, g_ref, wt_ref, b_ref, e_ref, gate_ref, cnt_ref):
    i = pl.program_id(0)
    h = _rms(x, g_ref[...], RMS_EPS)
    h_hi = h.astype(BF16)
    h_lo = (h - h_hi.astype(F32)).astype(BF16)
    nt = (((1,), (1,)), ((), ()))
    both = lax.dot_general(wt_ref[...], h_hi, nt, preferred_element_type=F32)
    cross = lax.dot_general(wt_ref[0:LOGIT_ROWS, :], h_lo, nt, preferred_element_type=F32)
    lt = both[0:LOGIT_ROWS] + both[LOGIT_ROWS:2 * LOGIT_ROWS] + cross + b_ref[...]
    g0, g1, g2, g3 = lt[0:1], lt[1:2], lt[2:3], lt[3:4]
    gmax = jnp.maximum(jnp.maximum(g0, g1), jnp.maximum(g2, g3))
    gidx = jnp.where(g0 == gmax, 0, jnp.where(g1 == gmax, 1, jnp.where(g2 == gmax, 2, 3)))
    gsum = jnp.exp(g0 - gmax) + jnp.exp(g1 - gmax) + jnp.exp(g2 - gmax) + jnp.exp(g3 - gmax)
    g_p = 1.0 / gsum
    E = EXPERTS_PER_GROUP
    e_in = jnp.where(gidx == 0, lt[8:8 + E],
                     jnp.where(gidx == 1, lt[8 + E:8 + 2 * E],
                               jnp.where(gidx == 2, lt[8 + 2 * E:8 + 3 * E], lt[8 + 3 * E:8 + 4 * E])))
    io = lax.broadcasted_iota(jnp.int32, e_in.shape, 0)
    v0 = jnp.max(e_in, axis=0, keepdims=True)
    i0 = jnp.min(jnp.where(e_in == v0, io, E), axis=0, keepdims=True)
    rest = jnp.where(io == i0, -jnp.inf, e_in)
    v1 = jnp.max(rest, axis=0, keepdims=True)
    i1 = jnp.min(jnp.where(rest == v1, io, E), axis=0, keepdims=True)
    ex = jnp.exp(v1 - v0)
    p0 = 1.0 / (1.0 + ex)
    e0 = gidx * E + i0
    e1 = gidx * E + i1
    e_ref[0:1, :] = e0
    e_ref[1:2, :] = e1
    gate_ref[0:1, :] = g_p * p0
    gate_ref[1:2, :] = g_p * (ex * p0)
    io32 = lax.broadcasted_iota(jnp.int32, (N_EXPERTS, x.shape[0]), 0)
    oh = (io32 == e0).astype(F32) + (io32 == e1).astype(F32)
    tile_cnt = jnp.sum(oh, axis=1, keepdims=True)

    @pl.when(i == 0)
    def _():
        cnt_ref[...] = jnp.zeros(cnt_ref.shape, F32)

    cnt_ref[...] += jnp.broadcast_to(tile_cnt, cnt_ref.shape)


def _router(x2d, g, wg, bg, we, be):
    N, D = x2d.shape
    TM = ROW_TILE
    operands, in_specs, out_shape, out_specs = _router_io(N, D, g, wg, bg, we, be)
    return pl.pallas_call(
        _router_kernel,
        out_shape=out_shape,
        grid=(N // TM,),
        in_specs=[pl.BlockSpec((TM, D), lambda i: (i, 0))] + in_specs,
        out_specs=out_specs,
        compiler_params=_cparams(("arbitrary",)),
        name="moe_router",
    )(x2d, *operands)


def _router_io(N, D, g, wg, bg, we, be):
    TM = ROW_TILE
    wt = jnp.zeros((LOGIT_ROWS, D), F32).at[0:N_GROUPS].set(wg.T).at[8:8 + N_EXPERTS].set(we.T)
    bt = jnp.zeros((LOGIT_ROWS, 1), F32).at[0:N_GROUPS, 0].set(bg).at[8:8 + N_EXPERTS, 0].set(be)
    wt_hi = wt.astype(BF16)
    wt_parts = jnp.concatenate([wt_hi, (wt - wt_hi.astype(F32)).astype(BF16)], axis=0)
    in_specs = [
        pl.BlockSpec((1, D), lambda i: (0, 0)),
        pl.BlockSpec((2 * LOGIT_ROWS, D), lambda i: (0, 0)),
        pl.BlockSpec((LOGIT_ROWS, 1), lambda i: (0, 0)),
    ]
    out_shape = (
        jax.ShapeDtypeStruct((2, N), jnp.int32),
        jax.ShapeDtypeStruct((2, N), F32),
        jax.ShapeDtypeStruct((N_EXPERTS, 128), F32),
    )
    out_specs = (
        pl.BlockSpec((2, TM), lambda i: (0, i)),
        pl.BlockSpec((2, TM), lambda i: (0, i)),
        pl.BlockSpec((N_EXPERTS, 128), lambda i: (0, 0)),
    )
    return (g.reshape(1, D), wt_parts, bt), in_specs, out_shape, out_specs


def _rank_kernel(e_ref, start_ref, tri_ref, low_ref, lpos_ref, rstart_ref, rlen_ref, roff_ref, carry_ref):
    i = pl.program_id(0)

    @pl.when(i == 0)
    def _():
        carry_ref[...] = jnp.zeros(carry_ref.shape, F32)

    TM = e_ref.shape[1]
    io32 = lax.broadcasted_iota(jnp.int32, (N_EXPERTS, TM), 0)
    base = carry_ref[...] + start_ref[...]
    oh = [io32 == e_ref[k:k + 1, :] for k in range(2)]
    ohf = [o.astype(F32) for o in oh]
    before = [jnp.dot(o.astype(BF16), tri_ref[...], preferred_element_type=F32) for o in ohf]
    cnt = [jnp.sum(o, axis=1, keepdims=True) for o in ohf]
    run_len = cnt[0] + cnt[1]
    lanes = rlen_ref.shape[2]
    run_off = jnp.dot(low_ref[...], jnp.broadcast_to(run_len, (N_EXPERTS, lanes)),
                      precision=lax.Precision.HIGHEST, preferred_element_type=F32)
    for k in range(2):
        in_run = before[k] + (cnt[0] if k == 1 else 0.0)
        lpos = jnp.sum(jnp.where(oh[k], in_run + run_off[:, 0:1], 0.0), axis=0, keepdims=True)
        lpos_ref[k:k + 1, :] = lpos.astype(jnp.int32)
    rstart_ref[0] = jnp.broadcast_to(base, (N_EXPERTS, lanes)).astype(jnp.int32)
    rlen_ref[0] = jnp.broadcast_to(run_len, (N_EXPERTS, lanes)).astype(jnp.int32)
    roff_ref[0] = run_off.astype(jnp.int32)
    carry_ref[...] = carry_ref[...] + run_len


def _rank(e_t, seg_start):
    _, N = e_t.shape
    TM = ROW_TILE
    n_tiles = N // TM
    tri = (jnp.arange(TM)[:, None] < jnp.arange(TM)[None, :]).astype(BF16)
    low = (jnp.arange(N_EXPERTS)[None, :] < jnp.arange(N_EXPERTS)[:, None]).astype(F32)
    tok = jax.ShapeDtypeStruct((2, N), jnp.int32)
    run = jax.ShapeDtypeStruct((n_tiles, N_EXPERTS, LANES), jnp.int32)
    tok_spec = pl.BlockSpec((2, TM), lambda i: (0, i))
    run_spec = pl.BlockSpec((1, N_EXPERTS, LANES), lambda i: (i, 0, 0))
    lpos_t, rstart, rlen, roff = pl.pallas_call(
        _rank_kernel,
        out_shape=(tok, run, run, run),
        grid=(n_tiles,),
        in_specs=[
            pl.BlockSpec((2, TM), lambda i: (0, i)),
            pl.BlockSpec((N_EXPERTS, 1), lambda i: (0, 0)),
            pl.BlockSpec((TM, TM), lambda i: (0, 0)),
            pl.BlockSpec((N_EXPERTS, N_EXPERTS), lambda i: (0, 0)),
        ],
        out_specs=(tok_spec, run_spec, run_spec, run_spec),
        scratch_shapes=[pltpu.VMEM((N_EXPERTS, 1), F32)],
        compiler_params=_cparams(("arbitrary",)),
        name="moe_rank",
    )(e_t, seg_start.astype(F32).reshape(N_EXPERTS, 1), tri, low)
    runs = tuple(r[:, :, 0].reshape(-1) for r in (rstart, rlen, roff))
    return lpos_t, runs


def _store_token_tiles(dst_ref, val):
    rows, d = val.shape
    nch = d // LANES
    for c in range(nch):
        dst_ref[pl.ds(c, rows, stride=nch), :] = val[:, c * LANES:(c + 1) * LANES]


def _load_token_tiles(src_ref, rows):
    nch = src_ref.shape[0] // rows
    return jnp.concatenate([src_ref[pl.ds(c, rows, stride=nch), :] for c in range(nch)], axis=1)


def _for_each_run_piece(rstart_ref, rlen_ref, roff_ref, tile, max_rows, row_align, emit):
    for e in range(N_EXPERTS):
        n = rlen_ref[tile * N_EXPERTS + e]
        first_pos = roff_ref[tile * N_EXPERTS + e]
        first_slot = rstart_ref[tile * N_EXPERTS + e]
        size = max_rows
        while size >= row_align:
            @pl.when((n & size) != 0)
            def _(first_pos=first_pos, first_slot=first_slot, size=size):
                emit(pl.multiple_of(first_pos, row_align), pl.multiple_of(first_slot, row_align), size)
            first_pos = first_pos + (n & size)
            first_slot = first_slot + (n & size)
            size //= 2


def _dispatch_kernel(rstart_ref, rlen_ref, roff_ref, fill_ref, npad_ref, nu_ref,
                     lpos_ref, x_ref, g_ref, xs_ref, hs_ref, sem, zsem):
    TM = ROW_TILE
    D = x_ref.shape[1]
    nch = D // LANES
    BLK = EXPERT_BLOCK
    n_blocks = xs_ref.shape[0] // (BLK * nch)
    i = pl.program_id(0)

    def wait_copies(buf):
        for _ in range(2):
            pltpu.make_async_copy(hs_ref.at[buf, pl.ds(0, TM * nch)], xs_ref.at[pl.ds(0, TM * nch)],
                                  sem.at[buf]).wait()

    for buf in range(DISPATCH_TILES):
        rows = slice(buf * TM, (buf + 1) * TM)
        tile = i * DISPATCH_TILES + buf
        h = _rms(x_ref[rows, :], g_ref[...], RMS_EPS).astype(BF16)
        pos = lax.broadcasted_iota(jnp.int32, (2 * TM, TM), 0)
        perm = jnp.where(pos == lpos_ref[0:1, rows], 1.0, jnp.where(pos == lpos_ref[1:2, rows], 1.0, 0.0))
        sorted_rows = jnp.dot(perm.astype(BF16), h, preferred_element_type=F32)

        @pl.when(i >= 1)
        def _(buf=buf):
            wait_copies(buf)

        _store_token_tiles(hs_ref.at[buf], sorted_rows)

        def put_piece(pos_row, slot_row, n_rows, buf=buf):
            pltpu.make_async_copy(hs_ref.at[buf, pl.ds(pos_row, n_rows)], xs_ref.at[pl.ds(slot_row, n_rows)],
                                  sem.at[buf]).start()

        _for_each_run_piece(rstart_ref, rlen_ref, roff_ref, tile, TM * nch, nch, put_piece)

    @pl.when(i == pl.num_programs(0) - 1)
    def _():
        for buf in range(DISPATCH_TILES):
            wait_copies(buf)
        hs_ref[0] = jnp.zeros(hs_ref.shape[1:], hs_ref.dtype)

        def zero_copy(first_slot, n_slots):
            return pltpu.make_async_copy(hs_ref.at[0, pl.ds(0, n_slots * nch)],
                                         xs_ref.at[pl.ds(pl.multiple_of(first_slot * nch, nch), n_slots * nch)], zsem)

        def pad_copies(act):
            for e in range(N_EXPERTS):
                first = fill_ref[e]
                n = npad_ref[e]
                size = BLK // 2
                while size >= 1:
                    @pl.when((n & size) != 0)
                    def _(first=first, size=size):
                        act(zero_copy(first, size))
                    first = first + (n & size)
                    size //= 2
            for b in range(N_EXPERTS):
                @pl.when(nu_ref[0] + b < n_blocks)
                def _(b=b):
                    act(zero_copy((nu_ref[0] + b) * BLK, BLK))

        pad_copies(lambda c: c.start())
        pad_copies(lambda c: c.wait())


def _dispatch(x2d, g, lpos_t, runs, n_slots, fill_start, n_pad, n_used):
    N, D = x2d.shape
    TM = ROW_TILE
    nch = D // LANES
    assert EXPERT_BLOCK <= 2 * TM
    T = DISPATCH_TILES
    return pl.pallas_call(
        _dispatch_kernel,
        out_shape=jax.ShapeDtypeStruct((n_slots * nch, LANES), F32),
        grid_spec=pltpu.PrefetchScalarGridSpec(
            num_scalar_prefetch=6,
            grid=(N // (T * TM),),
            in_specs=[
                pl.BlockSpec((2, T * TM), lambda i, *_: (0, i)),
                pl.BlockSpec((T * TM, D), lambda i, *_: (i, 0)),
                pl.BlockSpec((1, D), lambda i, *_: (0, 0)),
            ],
            out_specs=pl.BlockSpec(memory_space=pl.ANY),
            scratch_shapes=[pltpu.VMEM((T, 2 * TM * nch, LANES), F32), pltpu.SemaphoreType.DMA((T,)),
                            pltpu.SemaphoreType.DMA],
        ),
        compiler_params=_cparams(("arbitrary",)),
        name="moe_dispatch",
    )(*runs, fill_start, n_pad, n_used, lpos_t, x2d, g.reshape(1, D))


def _expert_kernel(be_ref, nu_ref, xs_ref, w1_ref, w3_ref, w2_ref, ys_ref, w13b, w2b):
    i = pl.program_id(0)
    BLK = EXPERT_BLOCK
    DE = w2b.shape[0]

    @pl.when(i < nu_ref[0])
    def _():
        prev = be_ref[jnp.maximum(i - 1, 0)]

        @pl.when((i == 0) | (be_ref[i] != prev))
        def _():
            w13b[:, 0:DE] = w1_ref[0].astype(BF16)
            w13b[:, DE:2 * DE] = w3_ref[0].astype(BF16)
            w2b[...] = w2_ref[0].astype(BF16)

        xb = _load_token_tiles(xs_ref, BLK).astype(BF16)
        ab = jnp.dot(xb, w13b[...], preferred_element_type=F32)
        a = ab[:, 0:DE]
        b = ab[:, DE:2 * DE]
        act = a * (1.0 / (1.0 + jnp.exp(-a))) * b
        _store_token_tiles(ys_ref, jnp.dot(act.astype(BF16), w2b[...], preferred_element_type=F32))

    @pl.when(i >= nu_ref[0])
    def _():
        ys_ref[...] = jnp.zeros(ys_ref.shape, F32)


def _experts(xs, block_expert, n_used, w1, w3, w2, layer):
    D, DE = w1.shape[2], w1.shape[3]
    nch = D // LANES
    BLK = EXPERT_BLOCK
    n_blocks = xs.shape[0] // (BLK * nch)

    def xmap(i, be, nu):
        return (jnp.minimum(i, jnp.maximum(nu[0] - 1, 0)), 0)

    def wmap(i, be, nu):
        return (layer, be[i], 0, 0)

    return pl.pallas_call(
        _expert_kernel,
        out_shape=jax.ShapeDtypeStruct(xs.shape, F32),
        grid_spec=pltpu.PrefetchScalarGridSpec(
            num_scalar_prefetch=2,
            grid=(n_blocks,),
            in_specs=[
                pl.BlockSpec((BLK * nch, LANES), xmap),
                pl.BlockSpec((None, 1, D, DE), wmap),
                pl.BlockSpec((None, 1, D, DE), wmap),
                pl.BlockSpec((None, 1, DE, D), wmap),
            ],
            out_specs=pl.BlockSpec((BLK * nch, LANES), lambda i, be, nu: (i, 0)),
            scratch_shapes=[
                pltpu.VMEM((D, 2 * DE), BF16),
                pltpu.VMEM((DE, D), BF16),
            ],
        ),
        compiler_params=_cparams(("arbitrary",)),
        name="moe_experts",
    )(block_expert, n_used, xs, w1, w3, w2)


def _combine_kernel(rstart_ref, rlen_ref, roff_ref, lpos_ref, x_ref, gate_ref, ys_ref, g_ref, o_ref,
                    sbuf, ybuf, sem, *, final_norm):
    TM = ROW_TILE
    D = x_ref.shape[1]
    nch = D // LANES
    i = pl.program_id(0)

    def gather_runs(tile, buf):
        def get_piece(pos_row, slot_row, n_rows):
            pltpu.make_async_copy(ys_ref.at[pl.ds(slot_row, n_rows)], sbuf.at[buf, pl.ds(pos_row, n_rows)],
                                  sem.at[buf]).start()

        _for_each_run_piece(rstart_ref, rlen_ref, roff_ref, tile, TM * nch, nch, get_piece)

    @pl.when(i == 0)
    def _():
        for buf in range(COMBINE_TILES):
            gather_runs(buf, buf)

    for buf in range(COMBINE_TILES):
        tile = i * COMBINE_TILES + buf
        rows = slice(buf * TM, (buf + 1) * TM)
        for _ in range(2):
            pltpu.make_async_copy(ys_ref.at[pl.ds(0, TM * nch)], sbuf.at[buf, pl.ds(0, TM * nch)],
                                  sem.at[buf]).wait()

        def to_token_order(t, carry, buf=buf):
            tok = pl.ds(pl.multiple_of(t * nch, nch), nch)
            packed = lpos_ref[buf * TM + t]
            pos0 = packed & 0xFFFF
            pos1 = lax.shift_right_logical(packed, 16)
            ybuf[0, tok, :] = sbuf[buf, pl.ds(pl.multiple_of(pos0 * nch, nch), nch), :]
            ybuf[1, tok, :] = sbuf[buf, pl.ds(pl.multiple_of(pos1 * nch, nch), nch), :]
            return carry

        lax.fori_loop(0, TM, to_token_order, 0, unroll=8)

        @pl.when(i + 1 < pl.num_programs(0))
        def _(tile=tile, buf=buf):
            gather_runs(tile + COMBINE_TILES, buf)

        gate = gate_ref[rows, :]
        y = (x_ref[rows, :] + gate[:, 0:1] * _load_token_tiles(ybuf.at[0], TM)
             + gate[:, 1:2] * _load_token_tiles(ybuf.at[1], TM))
        if final_norm:
            y = _rms(y, g_ref[...], RMS_EPS)
        o_ref[rows, :] = y


def _combine(x2d, lpos_t, gate_t, runs, ys, g, final_norm):
    N, D = x2d.shape
    TM = ROW_TILE
    T = COMBINE_TILES
    nch = D // LANES
    n_steps = N // (T * TM)
    assert 2 * TM <= 1 << 16
    packed_pos = lpos_t[0] | (lpos_t[1] << 16)
    gate = gate_t.T
    return pl.pallas_call(
        functools.partial(_combine_kernel, final_norm=final_norm),
        out_shape=jax.ShapeDtypeStruct((N, D), F32),
        grid_spec=pltpu.PrefetchScalarGridSpec(
            num_scalar_prefetch=3,
            grid=(n_steps,),
            in_specs=[
                pl.BlockSpec((T * TM,), lambda i, *_: (i,), memory_space=pltpu.SMEM),
                pl.BlockSpec((T * TM, D), lambda i, *_: (i, 0)),
                pl.BlockSpec((T * TM, 2), lambda i, *_: (i, 0)),
                pl.BlockSpec(memory_space=pl.ANY),
                pl.BlockSpec((1, D), lambda i, *_: (0, 0)),
            ],
            out_specs=pl.BlockSpec((T * TM, D), lambda i, *_: (i, 0)),
            scratch_shapes=[pltpu.VMEM((T, 2 * TM * nch, LANES), F32), pltpu.VMEM((2, TM * nch, LANES), F32),
                            pltpu.SemaphoreType.DMA((T,))],
        ),
        compiler_params=_cparams(("arbitrary",)),
        name="moe_combine",
    )(*runs, packed_pos, x2d, gate, ys, g.reshape(1, D))


def _moe_layer(x2d, norm_g, wg, bg, we, be, w1, w3, w2, layer, out_g, final_norm, routing=None):
    N, D = x2d.shape
    BLK = EXPERT_BLOCK
    n_slots = 2 * N + N_EXPERTS * BLK
    e_t, gate_t, cnt = routing if routing is not None else _router(x2d, norm_g, wg, bg, we, be)
    counts = cnt[:, 0].astype(jnp.int32)
    padded = ((counts + BLK - 1) // BLK) * BLK
    seg_end = jnp.cumsum(padded)
    seg_start = seg_end - padded
    n_blocks = n_slots // BLK
    block_row0 = jnp.arange(n_blocks, dtype=jnp.int32) * BLK
    block_expert = jnp.minimum(
        jnp.sum((seg_end[None, :] <= block_row0[:, None]).astype(jnp.int32), axis=1), N_EXPERTS - 1)
    n_used = (seg_end[-1:] // BLK).astype(jnp.int32)
    lpos_t, runs = _rank(e_t, seg_start)
    runs = tuple(r * (D // LANES) for r in runs)
    xs = _dispatch(x2d, norm_g, lpos_t, runs, n_slots, seg_start + counts, padded - counts, n_used)
    ys = _experts(xs, block_expert, n_used, w1, w3, w2, layer)
    return _combine(x2d, lpos_t, gate_t, runs, ys, out_g, final_norm)


def _qkv_kernel(x_ref, g_ref, wq_ref, wk_ref, wv_ref, q_ref, k_ref, v_ref):
    h = _rms(x_ref[...], g_ref[...], RMS_EPS).astype(BF16)
    q = jnp.dot(h, wq_ref[...], preferred_element_type=F32)
    q_ref[...] = (q * (LOG2E * HEAD_DIM ** -0.5)).astype(BF16)
    k_ref[...] = jnp.dot(h, wk_ref[...], preferred_element_type=F32).astype(BF16)
    v_ref[...] = jnp.dot(h, wv_ref[...], preferred_element_type=F32).astype(BF16)


def _qkv(x2d, g, wq, wk, wv):
    N, D = x2d.shape
    DA = wq.shape[1]
    TM = ROW_TILE
    wspec = pl.BlockSpec((D, DA), lambda i: (0, 0))
    ospec = pl.BlockSpec((TM, DA), lambda i: (i, 0))
    osh = jax.ShapeDtypeStruct((N, DA), BF16)
    return pl.pallas_call(
        _qkv_kernel,
        out_shape=(osh, osh, osh),
        grid=(N // TM,),
        in_specs=[pl.BlockSpec((TM, D), lambda i: (i, 0)), pl.BlockSpec((1, D), lambda i: (0, 0)),
                  wspec, wspec, wspec],
        out_specs=(ospec, ospec, ospec),
        compiler_params=_cparams(("arbitrary",)),
        name="attn_qkv",
    )(x2d, g.reshape(1, D), wq.astype(BF16), wk.astype(BF16), wv.astype(BF16))


def _attn_kernel(slope_ref, q_ref, k_ref, v_ref, db_ref, lam_ref, sg_ref, o_ref,
                 qz_ref, s_ref, p_ref, m_ref, l_ref, acc_ref, *, lam_init):
    TQ = q_ref.shape[0]
    TK = ATTN_TK
    hd = pl.program_id(1)
    qi = pl.program_id(2)
    slope = slope_ref[hd]

    q = q_ref[...]
    lane = lax.broadcasted_iota(jnp.int32, q.shape, 1)
    zero = jnp.zeros_like(q)
    qz_ref[0:TQ, :] = jnp.where(lane < HEAD_DIM, q, zero)
    qz_ref[TQ:2 * TQ, :] = jnp.where(lane >= HEAD_DIM, q, zero)
    m_ref[...] = jnp.full(m_ref.shape, -jnp.inf, F32)
    l_ref[...] = jnp.zeros(l_ref.shape, F32)
    acc_ref[...] = jnp.zeros(acc_ref.shape, F32)
    LW = acc_ref.shape[1]
    NC = TK // LW
    kpos = lax.broadcasted_iota(jnp.int32, (1, TK), 1)
    q0 = qi * TQ

    def tile(j, diagonal):
        koff = pl.multiple_of(j * TK, TK)
        s = lax.dot_general(qz_ref[...], k_ref[pl.ds(koff, TK), :], (((1,), (1,)), ((), ())),
                            preferred_element_type=F32)
        if diagonal:
            db = db_ref[qi % (TK // TQ)]
            s = jnp.concatenate([s[0:TQ] + db, s[TQ:2 * TQ] + db], axis=0)
        else:
            s = s + slope * (kpos + (j * TK - q0)).astype(F32)
        s_ref[...] = s
        mx = s[:, 0:LW]
        for c in range(1, NC):
            mx = jnp.maximum(mx, s[:, c * LW:(c + 1) * LW])
        m_old = m_ref[...]
        m_new = jnp.maximum(m_old, jnp.max(mx, axis=-1, keepdims=True))
        alpha = jnp.exp2(m_old - m_new)
        m_ref[...] = m_new
        psum = None
        for c in range(NC):
            p = jnp.exp2(s_ref[:, c * LW:(c + 1) * LW] - m_new)
            p_ref[:, c * LW:(c + 1) * LW] = p.astype(BF16)
            psum = p if psum is None else psum + p
        l_ref[...] = alpha * l_ref[...] + psum
        acc_ref[...] = alpha * acc_ref[...] + jnp.dot(p_ref[...], v_ref[pl.ds(koff, TK), :],
                                                      preferred_element_type=F32)

    def past_tile(j, carry):
        tile(j, False)
        return carry

    n_past = q0 // TK
    lax.fori_loop(0, n_past, past_tile, 0)
    tile(n_past, True)

    lamv = lam_ref[...]
    lam = (jnp.exp(jnp.sum(lamv[0:1] * lamv[1:2], axis=-1, keepdims=True))
           - jnp.exp(jnp.sum(lamv[2:3] * lamv[3:4], axis=-1, keepdims=True)) + lam_init)
    on = acc_ref[...] / jnp.sum(l_ref[...], axis=-1, keepdims=True)
    o = on[0:TQ] - lam * on[TQ:2 * TQ]
    o = _rms(o, sg_ref[...], SUBLN_EPS) * (1.0 - lam_init)
    o_ref[...] = o.astype(o_ref.dtype)


def _attention(q, k, v, lamv, subln_g, lam_init):
    B, S, DA = q.shape
    HW = 2 * HEAD_DIM
    TQ = ATTN_TQ
    TK = ATTN_TK
    assert TK % TQ == 0 and S % TK == 0 and TQ % CHUNK == 0
    n_par = TK // TQ
    slopes = jnp.array([LOG2E * 2.0 ** (-8.0 * (h + 1) / N_HEADS) for h in range(N_HEADS)], F32)
    qr = jnp.arange(TQ, dtype=jnp.int32)[None, :, None]
    kc = jnp.arange(TK, dtype=jnp.int32)[None, None, :] - jnp.arange(n_par, dtype=jnp.int32)[:, None, None] * TQ
    rel = jnp.where(kc <= qr, kc, 2 * qr - kc).astype(F32)
    allowed = jnp.floor_divide(kc, CHUNK) <= qr // CHUNK
    diag_bias = jnp.where(allowed[None], slopes[:, None, None, None] * rel[None], NEG_INF)
    return pl.pallas_call(
        functools.partial(_attn_kernel, lam_init=lam_init),
        out_shape=jax.ShapeDtypeStruct((B, S, DA), BF16),
        grid_spec=pltpu.PrefetchScalarGridSpec(
            num_scalar_prefetch=1,
            grid=(B, N_HEADS, S // TQ),
            in_specs=[
                pl.BlockSpec((None, TQ, HW), lambda b, h, i, sl: (b, i, h)),
                pl.BlockSpec((None, S, HW), lambda b, h, i, sl: (b, 0, h)),
                pl.BlockSpec((None, S, HW), lambda b, h, i, sl: (b, 0, h)),
                pl.BlockSpec((None, n_par, TQ, TK), lambda b, h, i, sl: (h, 0, 0, 0)),
                pl.BlockSpec((4, HEAD_DIM), lambda b, h, i, sl: (0, 0)),
                pl.BlockSpec((1, HW), lambda b, h, i, sl: (0, 0)),
            ],
            out_specs=pl.BlockSpec((None, TQ, HW), lambda b, h, i, sl: (b, i, h)),
            scratch_shapes=[
                pltpu.VMEM((2 * TQ, HW), BF16),
                pltpu.VMEM((2 * TQ, TK), F32),
                pltpu.VMEM((2 * TQ, TK), BF16),
                pltpu.VMEM((2 * TQ, HW), F32),
                pltpu.VMEM((2 * TQ, HW), F32),
                pltpu.VMEM((2 * TQ, HW), F32),
            ],
        ),
        compiler_params=_cparams(("arbitrary", "arbitrary", "arbitrary")),
        name="diff_attention",
    )(slopes, q, k, v, diag_bias, lamv, subln_g.reshape(1, HW))


def _wo_router_kernel(o_ref, w_ref, x_ref, g_ref, wt_ref, b_ref, out_ref, e_ref, gate_ref, cnt_ref):
    y = x_ref[...] + jnp.dot(o_ref[...], w_ref[...], preferred_element_type=F32)
    out_ref[...] = y
    _route(y, g_ref, wt_ref, b_ref, e_ref, gate_ref, cnt_ref)


def _wo_router(o2d, wo, x2d, router_params):
    N, D = x2d.shape
    DA = o2d.shape[1]
    TM = ROW_TILE
    operands, in_specs, out_shape, out_specs = _router_io(N, D, *router_params)
    out = pl.pallas_call(
        _wo_router_kernel,
        out_shape=(jax.ShapeDtypeStruct((N, D), F32),) + out_shape,
        grid=(N // TM,),
        in_specs=[pl.BlockSpec((TM, DA), lambda i: (i, 0)), pl.BlockSpec((DA, D), lambda i: (0, 0)),
                  pl.BlockSpec((TM, D), lambda i: (i, 0))] + in_specs,
        out_specs=(pl.BlockSpec((TM, D), lambda i: (i, 0)),) + out_specs,
        compiler_params=_cparams(("arbitrary",)),
        name="attn_wo_router",
    )(o2d, wo.astype(BF16), x2d, *operands)
    return out[0], out[1:]


def _attn_layer(x, g, wq, wk, wv, wo, lq1, lk1, lq2, lk2, subln_g, layer_idx, router_params):
    B, S, D = x.shape
    x2d = x.reshape(B * S, D)
    q, k, v = _qkv(x2d, g, wq, wk, wv)
    DA = q.shape[1]
    lam_init = 0.8 - 0.6 * math.exp(-0.3 * layer_idx)
    lamv = jnp.stack([lq1, lk1, lq2, lk2]).astype(F32)
    o = _attention(q.reshape(B, S, DA), k.reshape(B, S, DA), v.reshape(B, S, DA), lamv, subln_g, lam_init)
    y, routing = _wo_router(o.reshape(B * S, DA), wo, x2d, router_params)
    return y.reshape(B, S, D), routing


def kernel(x, norm_mix_g, norm_ffn_g, pool_w, pool_scale, attn_wq, attn_wk, attn_wv, attn_wo,
           attn_lq1, attn_lk1, attn_lq2, attn_lk2, attn_subln_g, moe_wg, moe_bg, moe_we, moe_be,
           moe_w1, moe_w3, moe_w2, final_g):
    B, S, D = x.shape
    depth = norm_mix_g.shape[0]
    for i in range(depth):
        j = i // 2
        router_params = (norm_ffn_g[i], moe_wg[i], moe_bg[i], moe_we[i], moe_be[i])
        routing = None
        if i % 2 == 0:
            x = _pool_layer(x, norm_mix_g[i], pool_w[j], pool_scale[j])
        else:
            x, routing = _attn_layer(x, norm_mix_g[i], attn_wq[j], attn_wk[j], attn_wv[j], attn_wo[j],
                                     attn_lq1[j], attn_lk1[j], attn_lq2[j], attn_lk2[j], attn_subln_g[j], i,
                                     router_params)
        last = i == depth - 1
        x = _moe_layer(x.reshape(B * S, D), *router_params, moe_w1, moe_w3, moe_w2, i, final_g, last,
                       routing).reshape(B, S, D)
    return x
```
